```python
import math
import jax, jax.numpy as jnp
from jax import lax
import numpy as np

D_MODEL = 2048
BATCH = 8
SEQ = 2048
DEPTH = 1

GRID_W = 64
CTX_LEN = 256
HEAD_DIM = 128
N_Q_HEADS = 8
N_KV_HEADS = 2
GQA_GROUP = N_Q_HEADS // N_KV_HEADS
ATTN_WIDTH = N_Q_HEADS * HEAD_DIM
KV_WIDTH = N_KV_HEADS * HEAD_DIM
Q_BLOCK = 128
ROPE_THETA = 10000.0
ROPE_AXIS_DIM = HEAD_DIM // 2
ROPE_FREQS = ROPE_AXIS_DIM // 2
F_GROUPS = 4
F_GROUP_DIM = 256
FOURIER_WIDTH = F_GROUPS * F_GROUP_DIM
MIX_WIDTH = ATTN_WIDTH + FOURIER_WIDTH
IN_PROJ_WIDTH = ATTN_WIDTH + 2 * KV_WIDTH + FOURIER_WIDTH
PEER_HEADS = 8
N_KEYS = 128
N_EXPERTS = N_KEYS * N_KEYS
D_KEY = 256
D_KEY_HALF = D_KEY // 2
TOPK_HALF = 16
TOPK = 16
PEER_BLOCK = 128
N_MOD = 6
EPS = 1e-6

kernel_name = "hymba_fnet_peer_dit_layer"


def rms_norm(x, g):
    xf = x.astype(jnp.float32)
    y = xf * lax.rsqrt(jnp.mean(xf * xf, axis=-1, keepdims=True) + EPS)
    return (y * g.astype(jnp.float32)).astype(x.dtype)


def modulate(h, shift, scale):
    return h * (1.0 + scale) + shift


def rope_tables(length):
    rows = length // GRID_W
    row = jnp.broadcast_to(jnp.arange(rows)[:, None], (rows, GRID_W)).reshape(-1)
    col = jnp.broadcast_to(jnp.arange(GRID_W)[None, :], (rows, GRID_W)).reshape(-1)
    inv_freq = ROPE_THETA ** (-jnp.arange(ROPE_FREQS, dtype=jnp.float32) / ROPE_FREQS)
    ang_row = row.astype(jnp.float32)[:, None] * inv_freq
    ang_col = col.astype(jnp.float32)[:, None] * inv_freq
    return ang_row, ang_col


def _rotate(xp, ang):
    c = jnp.cos(ang)[None, :, None, :].astype(xp.dtype)
    s = jnp.sin(ang)[None, :, None, :].astype(xp.dtype)
    x1, x2 = jnp.split(xp, 2, axis=-1)
    return jnp.concatenate([x1 * c - x2 * s, x2 * c + x1 * s], axis=-1)


def axial_rope(x, ang_row, ang_col):
    return jnp.concatenate([_rotate(x[..., :ROPE_AXIS_DIM], ang_row),
                            _rotate(x[..., ROPE_AXIS_DIM:], ang_col)], axis=-1)


def project(h, w_in, g_q, g_k):
    B, L, _ = h.shape
    p = h @ w_in
    q = p[..., :ATTN_WIDTH].reshape(B, L, N_Q_HEADS, HEAD_DIM)
    k = p[..., ATTN_WIDTH:ATTN_WIDTH + KV_WIDTH].reshape(B, L, N_KV_HEADS, HEAD_DIM)
    v = p[..., ATTN_WIDTH + KV_WIDTH:ATTN_WIDTH + 2 * KV_WIDTH].reshape(B, L, N_KV_HEADS, HEAD_DIM)
    f = p[..., ATTN_WIDTH + 2 * KV_WIDTH:]
    return rms_norm(q, g_q), rms_norm(k, g_k), v, f


def attend(q, k, v):
    B, Lq = q.shape[0], q.shape[1]
    nb = Lq // Q_BLOCK
    qb = (q * (HEAD_DIM ** -0.5)).reshape(B, nb, Q_BLOCK, N_KV_HEADS, GQA_GROUP, HEAD_DIM)
    qb = jnp.moveaxis(qb, 1, 0)

    def block(qi):
        s = jnp.einsum('bqgrd,bkgd->bgrqk', qi, k).astype(jnp.float32)
        p = jax.nn.softmax(s, axis=-1).astype(v.dtype)
        return jnp.einsum('bgrqk,bkgd->bqgrd', p, v)

    o = lax.map(block, qb)
    return jnp.moveaxis(o, 0, 1).reshape(B, Lq, ATTN_WIDTH)


def fourier_mix(f, w_f, b_f):
    B, L, _ = f.shape
    fg = f.reshape(B, L, F_GROUPS, F_GROUP_DIM).astype(jnp.float32)
    spec = jnp.fft.fft2(fg, axes=(1, 3), norm="ortho").real.astype(f.dtype)
    y = jnp.einsum('blgc,gcd->blgd', spec, w_f) + b_f.reshape(F_GROUPS, F_GROUP_DIM)
    return y.reshape(B, L, FOURIER_WIDTH)


def peer(h, w_query, sub_keys, u_exp, v_exp):
    B, L, D = h.shape
    hb = h.reshape(-1, PEER_BLOCK, D)

    def block(xb):
        q = (xb @ w_query).reshape(PEER_BLOCK, PEER_HEADS, 2, D_KEY_HALF)
        s = jnp.einsum('thpd,hpkd->thpk', q, sub_keys).astype(jnp.float32)
        s_top, i_top = lax.top_k(s, TOPK_HALF)
        cand = (s_top[:, :, 0, :, None] + s_top[:, :, 1, None, :]).reshape(
            PEER_BLOCK, PEER_HEADS, TOPK_HALF * TOPK_HALF)
        cand_idx = (i_top[:, :, 0, :, None] * N_KEYS + i_top[:, :, 1, None, :]).reshape(
            PEER_BLOCK, PEER_HEADS, TOPK_HALF * TOPK_HALF)
        s_fin, j = lax.top_k(cand, TOPK)
        idx = jnp.take_along_axis(cand_idx, j, axis=-1)
        g = jax.nn.softmax(s_fin, axis=-1)
        u_sel = u_exp[idx]
        a = jax.nn.gelu(jnp.einsum('td,thkd->thk', xb, u_sel), approximate=False)
        w = (g * a.astype(jnp.float32)).astype(xb.dtype)
        v_sel = v_exp[idx]
        return jnp.einsum('thk,thkd->td', w, v_sel)

    return lax.map(block, hb).reshape(B, L, D)


def setup_inputs(seed: int = 0) -> dict:
    key = jax.random.key(seed)
    ks = jax.random.split(key, 20)
    f32 = jnp.float32
    D = D_MODEL

    def nrm(k, shape, scale):
        return jax.random.normal(k, shape, f32) * scale

    return {
        "x": nrm(ks[0], (BATCH, SEQ, D), 1.0),
        "c": nrm(ks[1], (BATCH, D), 1.0),
        "ctx": nrm(ks[2], (BATCH, CTX_LEN, D), 1.0),
        "c_ctx": nrm(ks[3], (D,), 1.0),
        "w_ada": nrm(ks[4], (DEPTH, D, N_MOD * D), 0.5 * D ** -0.5),
        "b_ada": nrm(ks[5], (DEPTH, N_MOD * D), 0.02),
        "g_norm1": 1.0 + nrm(ks[6], (DEPTH, D), 0.02),
        "w_in": nrm(ks[7], (DEPTH, D, IN_PROJ_WIDTH), D ** -0.5),
        "g_q": 1.0 + nrm(ks[8], (DEPTH, HEAD_DIM), 0.02),
        "g_k": 1.0 + nrm(ks[9], (DEPTH, HEAD_DIM), 0.02),
        "w_fourier": nrm(ks[10], (DEPTH, F_GROUPS, F_GROUP_DIM, F_GROUP_DIM), F_GROUP_DIM ** -0.5),
        "b_fourier": nrm(ks[11], (DEPTH, FOURIER_WIDTH), 0.02),
        "w_out": nrm(ks[12], (DEPTH, MIX_WIDTH, D), MIX_WIDTH ** -0.5),
        "g_norm2": 1.0 + nrm(ks[13], (DEPTH, D), 0.02),
        "w_query": nrm(ks[14], (DEPTH, D, PEER_HEADS * D_KEY), D ** -0.5),
        "sub_keys": nrm(ks[15], (DEPTH, PEER_HEADS, 2, N_KEYS, D_KEY_HALF), D_KEY_HALF ** -0.5),
        "u_experts": nrm(ks[16], (DEPTH, N_EXPERTS, D), D ** -0.5),
        "v_experts": nrm(ks[17], (DEPTH, N_EXPERTS, D), 0.5),
        "g_final": 1.0 + nrm(ks[18], (D,), 0.02),
    }


def reference(x, c, ctx, c_ctx, w_ada, b_ada, g_norm1, w_in, g_q, g_k, w_fourier, b_fourier,
              w_out, g_norm2, w_query, sub_keys, u_experts, v_experts, g_final):
    L = x.shape[1]
    ang_row, ang_col = rope_tables(L)
    xl, xc = x, ctx
    for layer in range(DEPTH):
        mod_l = jax.nn.silu(c) @ w_ada[layer] + b_ada[layer]
        mod_c = jax.nn.silu(c_ctx) @ w_ada[layer] + b_ada[layer]
        sh1, sc1, gt1, sh2, sc2, gt2 = [m[:, None, :] for m in jnp.split(mod_l, N_MOD, axis=-1)]
        csh1, csc1, cgt1, csh2, csc2, cgt2 = jnp.split(mod_c, N_MOD, axis=-1)

        hl = modulate(rms_norm(xl, g_norm1[layer]), sh1, sc1)
        hc = modulate(rms_norm(xc, g_norm1[layer]), csh1, csc1)
        ql, kl, vl, fl = project(hl, w_in[layer], g_q[layer], g_k[layer])
        qc, kc, vc, fc = project(hc, w_in[layer], g_q[layer], g_k[layer])
        ql = axial_rope(ql, ang_row, ang_col)
        kl = axial_rope(kl, ang_row, ang_col)
        k_all = jnp.concatenate([kl, kc], axis=1)
        v_all = jnp.concatenate([vl, vc], axis=1)
        attn_l = attend(ql, k_all, v_all)
        four_l = fourier_mix(fl, w_fourier[layer], b_fourier[layer])
        xl = xl + gt1 * (jnp.concatenate([attn_l, four_l], axis=-1) @ w_out[layer])

        h2 = modulate(rms_norm(xl, g_norm2[layer]), sh2, sc2)
        xl = xl + gt2 * peer(h2, w_query[layer], sub_keys[layer], u_experts[layer], v_experts[layer])

        if layer < DEPTH - 1:
            attn_c = attend(qc, kc, vc)
            four_c = fourier_mix(fc, w_fourier[layer], b_fourier[layer])
            xc = xc + cgt1 * (jnp.concatenate([attn_c, four_c], axis=-1) @ w_out[layer])
            h2c = modulate(rms_norm(xc, g_norm2[layer]), csh2, csc2)
            xc = xc + cgt2 * peer(h2c, w_query[layer], sub_keys[layer], u_experts[layer], v_experts[layer])
    return rms_norm(xl, g_final)
```

```python
import functools
import math

import numpy as np
import jax
import jax.numpy as jnp
from jax import lax
from jax.experimental import pallas as pl
from jax.experimental.pallas import tpu as pltpu

GRID_W = 64
HEAD_DIM = 128
N_Q_HEADS = 8
N_KV_HEADS = 2
GQA_GROUP = N_Q_HEADS // N_KV_HEADS
ATTN_WIDTH = N_Q_HEADS * HEAD_DIM
KV_WIDTH = N_KV_HEADS * HEAD_DIM
ROPE_THETA = 10000.0
ROPE_FREQS = HEAD_DIM // 4
F_GROUPS = 4
PEER_HEADS = 8
TOPK = 16
N_MOD = 6
EPS = 1e-6

LANES = 128
VMEM_LIMIT = 56 * 1024 * 1024

F32 = jnp.float32
BF16 = jnp.bfloat16
NT_DIMS = (((1,), (1,)), ((), ()))


def _params(*sem):
    return pltpu.CompilerParams(dimension_semantics=sem, vmem_limit_bytes=VMEM_LIMIT)


def _resident(shape, index_map):
    return pl.BlockSpec(shape, index_map, pipeline_mode=pl.Buffered(1))


def _rms(x, g):
    return x * lax.rsqrt(jnp.mean(x * x, axis=-1, keepdims=True) + EPS) * g


def _ada_kernel(c_ref, w_ref, b_ref, o_ref):
    c = c_ref[...]
    a = (c * jax.nn.sigmoid(c)).astype(BF16)
    o_ref[...] = jnp.dot(a, w_ref[...].astype(BF16), preferred_element_type=F32) + b_ref[...]


def _ada(cc, w, b):
    rows, d = cc.shape
    n = w.shape[1]
    tn = next(t for t in (1024, 512, 256, LANES) if n % t == 0)
    return pl.pallas_call(
        _ada_kernel,
        grid=(n // tn,),
        in_specs=[pl.BlockSpec((rows, d), lambda j: (0, 0)),
                  pl.BlockSpec((d, tn), lambda j: (0, j)),
                  pl.BlockSpec((1, tn), lambda j: (0, j))],
        out_specs=pl.BlockSpec((rows, tn), lambda j: (0, j)),
        out_shape=jax.ShapeDtypeStruct((rows, n), F32),
        compiler_params=_params("parallel"),
        name="ada",
    )(cc, w, b.reshape(1, n))


def _rope(t, cos, sin_signed):
    lane = lax.broadcasted_iota(jnp.int32, t.shape, 1)
    partner = jnp.where((lane % 64) < 32, pltpu.roll(t, 96, 1), pltpu.roll(t, 32, 1))
    return t * cos + partner * sin_signed


def _inproj_kernel(x_ref, g1_ref, sh_ref, sc_ref, w_ref, gq_ref, gk_ref, cos_ref, sin_ref,
                   *out_refs, latent):
    h = (_rms(x_ref[0], g1_ref[...]) * (1.0 + sc_ref[0]) + sh_ref[0]).astype(BF16)
    if latent:
        q_ref, k_ref, v_ref, f_ref = out_refs
        cos, sin = cos_ref[...], sin_ref[...]
        q = jnp.dot(h, w_ref[:, 0:ATTN_WIDTH], preferred_element_type=F32)
        scale = HEAD_DIM ** -0.5
        for j in range(N_Q_HEADS):
            sl = slice(j * HEAD_DIM, (j + 1) * HEAD_DIM)
            q_ref[0, :, sl] = (_rope(_rms(q[:, sl], gq_ref[...]), cos, sin) * scale).astype(BF16)
        k0 = ATTN_WIDTH
    else:
        k_ref, v_ref = out_refs
        k0 = 0
    k = jnp.dot(h, w_ref[:, k0:k0 + KV_WIDTH], preferred_element_type=F32)
    for j in range(N_KV_HEADS):
        sl = slice(j * HEAD_DIM, (j + 1) * HEAD_DIM)
        t = _rms(k[:, sl], gk_ref[...])
        if latent:
            t = _rope(t, cos, sin)
        k_ref[0, :, sl] = t.astype(BF16)
    v_ref[0] = jnp.dot(h, w_ref[:, k0 + KV_WIDTH:k0 + 2 * KV_WIDTH],
                       preferred_element_type=F32).astype(BF16)
    if latent:
        f_ref[0] = jnp.dot(h, w_ref[:, ATTN_WIDTH + 2 * KV_WIDTH:],
                           preferred_element_type=F32).astype(BF16)


def _inproj(x, g1, sh, sc, w_bf, gq, gk, cos, sin, latent):
    b, l, d = x.shape
    tm = min(l, 512)
    wcols = w_bf.shape[1]
    fw = wcols - ATTN_WIDTH - 2 * KV_WIDTH
    row = lambda bi, i: (bi, i, 0)
    per_b = pl.BlockSpec((1, 1, d), lambda bi, i: (bi, 0, 0))
    vec = lambda n: pl.BlockSpec((1, n), lambda bi, i: (0, 0))
    if latent:
        w_spec = _resident((d, wcols), lambda bi, i: (0, 0))
        widths = (ATTN_WIDTH, KV_WIDTH, KV_WIDTH, fw)
    else:
        assert ATTN_WIDTH % (2 * KV_WIDTH) == 0
        w_spec = _resident((d, 2 * KV_WIDTH), lambda bi, i: (0, ATTN_WIDTH // (2 * KV_WIDTH)))
        widths = (KV_WIDTH, KV_WIDTH)
    return pl.pallas_call(
        functools.partial(_inproj_kernel, latent=latent),
        grid=(b, l // tm),
        in_specs=[pl.BlockSpec((1, tm, d), row), vec(d), per_b, per_b, w_spec,
                  vec(HEAD_DIM), vec(HEAD_DIM),
                  pl.BlockSpec((tm, HEAD_DIM), lambda bi, i: (i, 0)),
                  pl.BlockSpec((tm, HEAD_DIM), lambda bi, i: (i, 0))],
        out_specs=[pl.BlockSpec((1, tm, n), row) for n in widths],
        out_shape=[jax.ShapeDtypeStruct((b, l, n), BF16) for n in widths],
        compiler_params=_params("parallel", "parallel"),
        name="inproj_latent" if latent else "inproj_ctx",
    )(x, g1, sh, sc, w_bf, gq, gk, cos, sin)


def _attn_kernel(q_ref, k_ref, v_ref, o_ref):
    k = k_ref[0]
    v = v_ref[0]
    for j in range(GQA_GROUP):
        sl = slice(j * HEAD_DIM, (j + 1) * HEAD_DIM)
        s = lax.dot_general(q_ref[0, :, sl], k, NT_DIMS, preferred_element_type=F32)
        p = jnp.exp(s - jnp.max(s, axis=-1, keepdims=True))
        denom = jnp.sum(p, axis=-1, keepdims=True)
        o = jnp.dot(p.astype(BF16), v, preferred_element_type=F32) / denom
        o_ref[0, :, sl] = o.astype(BF16)


def _attention(q, k_all, v_all):
    b, l, _ = q.shape
    lk = k_all.shape[1]
    tq = min(l, 256)
    gw = GQA_GROUP * HEAD_DIM
    return pl.pallas_call(
        _attn_kernel,
        grid=(b, N_KV_HEADS, l // tq),
        in_specs=[pl.BlockSpec((1, tq, gw), lambda bi, g, i: (bi, i, g)),
                  pl.BlockSpec((1, lk, HEAD_DIM), lambda bi, g, i: (bi, 0, g)),
                  pl.BlockSpec((1, lk, HEAD_DIM), lambda bi, g, i: (bi, 0, g))],
        out_specs=pl.BlockSpec((1, tq, gw), lambda bi, g, i: (bi, i, g)),
        out_shape=jax.ShapeDtypeStruct((b, l, ATTN_WIDTH), BF16),
        compiler_params=_params("parallel", "parallel", "parallel"),
        name="attn",
    )(q, k_all, v_all)


def _dft_tables(l, c):
    kl = (np.arange(l, dtype=np.int64)[:, None] * np.arange(l, dtype=np.int64)[None, :]) % l
    ang = 2.0 * np.pi * kl.astype(np.float64) / l
    pos = np.concatenate([np.cos(ang), -np.sin(ang)], axis=1)
    jc = (np.arange(c, dtype=np.int64)[:, None] * np.arange(c, dtype=np.int64)[None, :]) % c
    angc = 2.0 * np.pi * jc.astype(np.float64) / c
    chan = np.stack([np.cos(angc), np.sin(angc)]) / math.sqrt(l * c)

    def split(t):
        head = t.astype(np.float32)
        return jnp.asarray(head) + jnp.asarray((t - head).astype(np.float32))

    return split(pos).astype(BF16), split(chan)


def _fmat_kernel(cs_ref, w_ref, o_ref):
    c = w_ref.shape[-1]
    w = w_ref[0]
    o_ref[0, :, 0:c] = jnp.dot(cs_ref[0], w, preferred_element_type=F32,
                               precision=lax.Precision.HIGHEST).astype(BF16)
    o_ref[0, :, c:2 * c] = jnp.dot(cs_ref[1], w, preferred_element_type=F32,
                                   precision=lax.Precision.HIGHEST).astype(BF16)


def _fmat(chan, w_f):
    g, c, _ = w_f.shape
    return pl.pallas_call(
        _fmat_kernel,
        grid=(g,),
        in_specs=[pl.BlockSpec((2, c, c), lambda gi: (0, 0, 0)),
                  pl.BlockSpec((1, c, c), lambda gi: (gi, 0, 0))],
        out_specs=pl.BlockSpec((1, c, 2 * c), lambda gi: (gi, 0, 0)),
        out_shape=jax.ShapeDtypeStruct((g, c, 2 * c), BF16),
        compiler_params=_params("parallel"),
        name="fmat",
    )(chan, w_f)


def _fourier_kernel(x_ref, m_ref, dft_ref, b_ref, o_ref, z_ref, *, tr):
    l, c = x_ref.shape[1], x_ref.shape[2]
    z = jnp.dot(x_ref[0], m_ref[0], preferred_element_type=F32)
    z_ref[0:l, :] = z[:, 0:c].astype(BF16)
    z_ref[l:2 * l, :] = z[:, c:2 * c].astype(BF16)
    for r in range(l // tr):
        rows = slice(r * tr, (r + 1) * tr)
        y = jnp.dot(dft_ref[rows, :], z_ref[...], preferred_element_type=F32) + b_ref[0]
        o_ref[0, rows, :] = y.astype(BF16)


def _fourier(f, fm, dft, b_f):
    b, l, fw = f.shape
    c = fw // F_GROUPS
    tr = min(l, 512)
    return pl.pallas_call(
        functools.partial(_fourier_kernel, tr=tr),
        grid=(b, F_GROUPS),
        in_specs=[pl.BlockSpec((1, l, c), lambda bi, g: (bi, 0, g)),
                  pl.BlockSpec((1, c, 2 * c), lambda bi, g: (g, 0, 0)),
                  _resident((l, 2 * l), lambda bi, g: (0, 0)),
                  pl.BlockSpec((1, 1, c), lambda bi, g: (g, 0, 0))],
        out_specs=pl.BlockSpec((1, l, c), lambda bi, g: (bi, 0, g)),
        out_shape=jax.ShapeDtypeStruct((b, l, fw), BF16),
        scratch_shapes=[pltpu.VMEM((2 * l, c), BF16)],
        compiler_params=_params("parallel", "parallel"),
        name="fourier",
    )(f, fm, dft, b_f.reshape(F_GROUPS, 1, c))


def _outproj_kernel(a_ref, f_ref, x_ref, wo_ref, gt1_ref, g2_ref, sh2_ref, sc2_ref, wq_ref,
                    xl_ref, h2_ref, qp_ref):
    aw = a_ref.shape[2]
    y = jnp.dot(a_ref[0], wo_ref[0:aw, :], preferred_element_type=F32)
    y = y + jnp.dot(f_ref[0], wo_ref[aw:, :], preferred_element_type=F32)
    xl = x_ref[0] + gt1_ref[0] * y
    xl_ref[0] = xl
    h2 = _rms(xl, g2_ref[...]) * (1.0 + sc2_ref[0]) + sh2_ref[0]
    h2_ref[0] = h2
    qp = jnp.dot(h2.astype(BF16), wq_ref[...], preferred_element_type=F32)
    for j in range(qp_ref.shape[0]):
        qp_ref[j] = qp[:, j * LANES:(j + 1) * LANES].astype(BF16)


def _outproj(attn, four, x, wo_bf, gt1, g2, sh2, sc2, wq_bf):
    b, l, d = x.shape
    tm = min(l, 256)
    nt = l // tm
    aw, fw = attn.shape[2], four.shape[2]
    qw = wq_bf.shape[1]
    row = lambda bi, i: (bi, i, 0)
    per_b = pl.BlockSpec((1, 1, d), lambda bi, i: (bi, 0, 0))
    return pl.pallas_call(
        _outproj_kernel,
        grid=(b, nt),
        in_specs=[pl.BlockSpec((1, tm, aw), row), pl.BlockSpec((1, tm, fw), row),
                  pl.BlockSpec((1, tm, d), row),
                  _resident((aw + fw, d), lambda bi, i: (0, 0)),
                  per_b, pl.BlockSpec((1, d), lambda bi, i: (0, 0)), per_b, per_b,
                  _resident((d, qw), lambda bi, i: (0, 0))],
        out_specs=[pl.BlockSpec((1, tm, d), row), pl.BlockSpec((1, tm, d), row),
                   pl.BlockSpec((qw // LANES, tm, LANES), lambda bi, i: (0, bi * nt + i, 0))],
        out_shape=[jax.ShapeDtypeStruct((b, l, d), F32), jax.ShapeDtypeStruct((b, l, d), F32),
                   jax.ShapeDtypeStruct((qw // LANES, b * l, LANES), BF16)],
        compiler_params=_params("parallel", "parallel"),
        name="outproj",
    )(attn, four, x, wo_bf, gt1, g2, sh2, sc2, wq_bf)


def _topk_rows(s, k, payload=None):
    rows = s.shape[0]
    iota = lax.broadcasted_iota(jnp.int32, s.shape, 0).astype(F32)
    vals, sel = [], []
    for _ in range(k):
        m = jnp.max(s, axis=0, keepdims=True)
        i = jnp.min(jnp.where(s == m, iota, float(rows)), axis=0, keepdims=True)
        hit = iota == i
        vals.append(m)
        sel.append(i if payload is None else
                   jnp.sum(jnp.where(hit, payload, 0.0), axis=0, keepdims=True))
        s = jnp.where(hit, -jnp.inf, s)
    return jnp.concatenate(vals, axis=0), jnp.concatenate(sel, axis=0)


def _pair_candidates(s1, i1, s2, i2, n_keys):
    assert TOPK == 16
    sub = lax.broadcasted_iota(jnp.int32, (8, s1.shape[1]), 0)
    ids = [i1[0:1] * n_keys + i2, i1[1:2] * n_keys + i2[0:8]]
    vals = [s1[0:1] + s2, s1[1:2] + s2[0:8]]
    for a in range(2, 8):
        keep = sub < TOPK // (a + 1)
        vals.append(jnp.where(keep, s1[a:a + 1] + s2[0:8], -jnp.inf))
        ids.append(i1[a:a + 1] * n_keys + i2[0:8])
    vals.append(s1[8:TOPK] + s2[0:1])
    ids.append(i1[8:TOPK] * n_keys + i2[0:1])
    return jnp.concatenate(vals, axis=0), jnp.concatenate(ids, axis=0)


def _retrieve_kernel(q_ref, keys_ref, g_ref, idx_ref, gs_ref, is_ref):
    n_keys = keys_ref.shape[2]

    def head(h, carry):
        halves = []
        for p in range(2):
            s = lax.dot_general(keys_ref[h, p], q_ref[2 * h + p], NT_DIMS,
                                preferred_element_type=F32)
            halves.append(_topk_rows(s, TOPK))
        (s1, i1), (s2, i2) = halves
        cand, cidx = _pair_candidates(s1, i1, s2, i2, float(n_keys))
        sf, eidx = _topk_rows(cand, TOPK, payload=cidx)
        e = jnp.exp(sf - sf[0:1])
        rows = pl.ds(pl.multiple_of(h * TOPK, TOPK), TOPK)
        gs_ref[rows, :] = e / jnp.sum(e, axis=0, keepdims=True)
        is_ref[rows, :] = eidx.astype(jnp.int32)
        return carry

    lax.fori_loop(0, PEER_HEADS, head, 0)
    g_ref[...] = gs_ref[...].T
    idx_ref[0] = is_ref[...]


def _retrieve(qp, keys_bf, tb):
    nq, n, _ = qp.shape
    npick = PEER_HEADS * TOPK
    assert tb == LANES and npick == LANES
    nblk = n // tb
    return pl.pallas_call(
        _retrieve_kernel,
        grid=(nblk,),
        in_specs=[pl.BlockSpec((nq, tb, LANES), lambda i: (0, i, 0)),
                  pl.BlockSpec(keys_bf.shape, lambda i: (0, 0, 0, 0))],
        out_specs=[pl.BlockSpec((tb, npick), lambda i: (i, 0)),
                   pl.BlockSpec((1, npick, tb), lambda i: (i, 0, 0))],
        out_shape=[jax.ShapeDtypeStruct((n, npick), F32),
                   jax.ShapeDtypeStruct((nblk, npick, tb), jnp.int32)],
        scratch_shapes=[pltpu.VMEM((npick, tb), F32), pltpu.VMEM((npick, tb), jnp.int32)],
        compiler_params=_params("parallel"),
        name="retrieve",
    )(qp, keys_bf)


N_SLOTS = 4
BITREV3 = (0, 4, 2, 6, 1, 5, 3, 7)


def _peer_kernel(idx_hbm, uv_hbm, g_ref, h2_ref, xl_ref, gt2_ref, gf_ref, o_ref,
                 idx_smem, idx_sem, sem, wb_ref, peer_ref, *bufs):
    tb, d = h2_ref.shape
    npick = g_ref.shape[1]
    nc = d // LANES
    blk_words = npick * tb
    i = pl.program_id(0)
    nblk = pl.num_programs(0)
    cur = i % 2
    nxt = 1 - cur
    has_next = i + 1 < nblk
    ahead = N_SLOTS - 1

    def idx_copy(blk, half):
        return pltpu.make_async_copy(
            idx_hbm.at[pl.ds(pl.multiple_of(blk * blk_words, blk_words), blk_words)],
            idx_smem.at[pl.ds(pl.multiple_of(half * blk_words, blk_words), blk_words)],
            idx_sem.at[half])

    def slab_copy(word, k, slot):
        return pltpu.make_async_copy(uv_hbm.at[idx_smem[word]], bufs[slot].at[k], sem.at[slot])

    def wait(slot):
        pltpu.make_async_copy(uv_hbm.at[pl.ds(0, npick)], bufs[slot], sem.at[slot]).wait()

    sub = lax.broadcasted_iota(jnp.int32, (8, LANES), 0)

    def fold(a, b, dist):
        m = (sub & dist) == 0
        return (jnp.where(m, a, b)
                + jnp.where(m, pltpu.roll(a, 8 - dist, 0), pltpu.roll(b, dist, 0)))

    def compute(t, slot, word0, fill):
        def start_copy(k):
            slab_copy(word0 + k * tb, k, fill).start()

        xrow = h2_ref[pl.ds(t, 1), :]
        x = jnp.concatenate([xrow[:, c * LANES:(c + 1) * LANES] for c in range(nc)], axis=0)

        def partial(k):
            p = bufs[slot][k, 0:nc, :].astype(F32) * x
            acc = p[0:8]
            for r in range(1, nc // 8):
                acc = acc + p[8 * r:8 * r + 8]
            return acc

        rows = []
        for grp in range(npick // 8):
            for j in range(4):
                start_copy(grp * 4 + j)
            ps = [partial(grp * 8 + BITREV3[j]) for j in range(8)]
            q = [fold(ps[2 * j], ps[2 * j + 1], 4) for j in range(4)]
            rows.append(fold(fold(q[0], q[1], 2), fold(q[2], q[3], 2), 1))
        part = jnp.concatenate(rows, axis=0)
        a = jnp.sum(part.T, axis=0, keepdims=True)
        gelu = 0.5 * a * (1.0 + lax.erf(a * math.sqrt(0.5)))
        w = g_ref[pl.ds(t, 1), :] * gelu
        wb_ref[...] = jnp.broadcast_to(w, (npick, npick)).T

        accs = [None] * 4
        for k in range(npick):
            if k % 2 == 0:
                start_copy(npick // 2 + k // 2)
            term = wb_ref[k:k + 1, :] * bufs[slot][k, nc:2 * nc, :].astype(F32)
            accs[k % 4] = term if accs[k % 4] is None else accs[k % 4] + term
        out = (accs[0] + accs[1]) + (accs[2] + accs[3])

        group = pl.ds(pl.multiple_of((t // 8) * 8, 8), 8)
        mine = sub == t % 8
        for c in range(nc):
            cols = slice(c * LANES, (c + 1) * LANES)
            peer_ref[group, cols] = jnp.where(mine, out[c:c + 1, :], peer_ref[group, cols])

    @pl.when(i == 0)
    def _():
        first = idx_copy(0, 0)
        first.start()
        first.wait()
        for t in range(ahead):
            for k in range(npick):
                slab_copy(k * tb + t, k, t).start()

    @pl.when(has_next)
    def _():
        idx_copy(i + 1, nxt).start()

    peer_ref[...] = jnp.zeros_like(peer_ref)

    next_base = jnp.where(has_next, nxt, cur) * blk_words - tb

    last = tb // N_SLOTS - 1

    def group_of_tokens(j, carry):
        @pl.when(jnp.logical_and(j == last, has_next))
        def _():
            idx_copy(i + 1, nxt).wait()

        for u in range(N_SLOTS):
            t = j * N_SLOTS + u
            ta = t + ahead
            word0 = jnp.where(ta < tb, cur * blk_words, next_base) + ta
            wait(u)
            compute(t, u, word0, (u + ahead) % N_SLOTS)
        return carry

    lax.fori_loop(0, last + 1, group_of_tokens, 0)

    @pl.when(jnp.logical_not(has_next))
    def _():
        for s in range(ahead):
            wait(s)

    y = xl_ref[...] + gt2_ref[0] * peer_ref[...]
    o_ref[...] = _rms(y, gf_ref[...])


def _peer(idx_t, uv, g, h2, xl, gt2, gf, blocks_per_batch):
    nblk, npick, tb = idx_t.shape
    n, d = h2.shape
    nc = d // LANES
    assert tb % N_SLOTS == 0 and npick % 8 == 0 and nc % 16 == 0
    assert uv.shape[1:] == (2 * nc, LANES) and uv.dtype == BF16
    row = lambda i: (i, 0)
    return pl.pallas_call(
        _peer_kernel,
        grid=(nblk,),
        in_specs=[pl.BlockSpec(memory_space=pl.ANY), pl.BlockSpec(memory_space=pl.ANY),
                  pl.BlockSpec((tb, npick), row), pl.BlockSpec((tb, d), row),
                  pl.BlockSpec((tb, d), row),
                  pl.BlockSpec((1, 1, d), lambda i: (i // blocks_per_batch, 0, 0)),
                  pl.BlockSpec((1, d), lambda i: (0, 0))],
        out_specs=pl.BlockSpec((tb, d), row),
        out_shape=jax.ShapeDtypeStruct((n, d), F32),
        scratch_shapes=[pltpu.SMEM((2 * npick * tb,), jnp.int32), pltpu.SemaphoreType.DMA((2,)),
                        pltpu.SemaphoreType.DMA((N_SLOTS,)),
                        pltpu.VMEM((npick, npick), F32), pltpu.VMEM((tb, d), F32)]
        + [pltpu.VMEM((npick, 2 * nc, LANES), BF16) for _ in range(N_SLOTS)],
        compiler_params=_params("arbitrary"),
        name="peer",
    )(idx_t.reshape(-1), uv, g, h2, xl, gt2, gf)


def _rope_tables(length):
    rows = length // GRID_W
    row = jnp.broadcast_to(jnp.arange(rows)[:, None], (rows, GRID_W)).reshape(-1)
    col = jnp.broadcast_to(jnp.arange(GRID_W)[None, :], (rows, GRID_W)).reshape(-1)
    inv_freq = ROPE_THETA ** (-jnp.arange(ROPE_FREQS, dtype=F32) / ROPE_FREQS)
    ar = row.astype(F32)[:, None] * inv_freq
    ac = col.astype(F32)[:, None] * inv_freq
    cos = jnp.concatenate([jnp.cos(ar), jnp.cos(ar), jnp.cos(ac), jnp.cos(ac)], axis=-1)
    sin = jnp.concatenate([-jnp.sin(ar), jnp.sin(ar), -jnp.sin(ac), jnp.sin(ac)], axis=-1)
    return cos, sin


def kernel(x, c, ctx, c_ctx, w_ada, b_ada, g_norm1, w_in, g_q, g_k, w_fourier, b_fourier,
           w_out, g_norm2, w_query, sub_keys, u_experts, v_experts, g_final):
    b, l, d = x.shape
    assert w_ada.shape[0] == 1, "single-layer configuration only"
    layer = 0
    tb = LANES

    rows = -(-(b + 1) // 8) * 8
    cc = jnp.concatenate([c, c_ctx[None, :], jnp.zeros((rows - b - 1, d), F32)], axis=0)
    mod = _ada(cc, w_ada[layer], b_ada[layer])
    sh1, sc1, gt1, sh2, sc2, gt2 = [m.reshape(b, 1, d) for m in jnp.split(mod[:b], N_MOD, axis=-1)]
    csh1, csc1 = [jnp.broadcast_to(m.reshape(1, 1, d), (b, 1, d))
                  for m in jnp.split(mod[b], N_MOD, axis=-1)[:2]]

    cos, sin = _rope_tables(l)
    g1 = g_norm1[layer].reshape(1, d)
    gq = g_q[layer].reshape(1, HEAD_DIM)
    gk = g_k[layer].reshape(1, HEAD_DIM)
    w_in_bf = w_in[layer].astype(BF16)
    q, k_l, v_l, f = _inproj(x, g1, sh1, sc1, w_in_bf, gq, gk, cos, sin, latent=True)
    lc = ctx.shape[1]
    k_c, v_c = _inproj(ctx, g1, csh1, csc1, w_in_bf, gq, gk, cos[:lc], sin[:lc], latent=False)

    attn = _attention(q, jnp.concatenate([k_l, k_c], axis=1), jnp.concatenate([v_l, v_c], axis=1))

    dft, chan = _dft_tables(l, f.shape[2] // F_GROUPS)
    four = _fourier(f, _fmat(chan, w_fourier[layer]), dft, b_fourier[layer])

    xl, h2, qp = _outproj(attn, four, x, w_out[layer].astype(BF16), gt1,
                          g_norm2[layer].reshape(1, d), sh2, sc2, w_query[layer].astype(BF16))

    g, idx_t = _retrieve(qp, sub_keys[layer].astype(BF16), tb)
    n_exp = u_experts.shape[1]
    uv = jnp.concatenate([u_experts[layer].reshape(n_exp, d // LANES, LANES),
                          v_experts[layer].reshape(n_exp, d // LANES, LANES)], axis=1).astype(BF16)
    out = _peer(idx_t, uv, g, h2.reshape(b * l, d), xl.reshape(b * l, d), gt2,
                g_final.reshape(1, d), l // tb)
    return out.reshape(b, l, d)
```

```python
import functools
import math

import numpy as np
import jax
import jax.numpy as jnp
from jax import lax
from jax.experimental import pallas as pl
from jax.experimental.pallas import tpu as pltpu

GRID_W = 64
HEAD_DIM = 128
N_Q_HEADS = 8
N_KV_HEADS = 2
GQA_GROUP = N_Q_HEADS // N_KV_HEADS
ATTN_WIDTH = N_Q_HEADS * HEAD_DIM
KV_WIDTH = N_KV_HEADS * HEAD_DIM
ROPE_THETA = 10000.0
ROPE_FREQS = HEAD_DIM // 4
F_GROUPS = 4
PEER_HEADS = 8
TOPK = 16
N_MOD = 6
EPS = 1e-6

LANES = 128
VMEM_LIMIT = 56 * 1024 * 1024

F32 = jnp.float32
BF16 = jnp.bfloat16
NT_DIMS = (((1,), (1,)), ((), ()))


def _params(*sem):
    return pltpu.CompilerParams(dimension_semantics=sem, vmem_limit_bytes=VMEM_LIMIT)


def _resident(shape, index_map):
    return pl.BlockSpec(shape, index_map, pipeline_mode=pl.Buffered(1))


def _rms(x, g):
    return x * lax.rsqrt(jnp.mean(x * x, axis=-1, keepdims=True) + EPS) * g


def _ada_kernel(c_ref, w_ref, b_ref, o_ref):
    c = c_ref[...]
    a = (c * jax.nn.sigmoid(c)).astype(BF16)
    o_ref[...] = jnp.dot(a, w_ref[...].astype(BF16), preferred_element_type=F32) + b_ref[...]


def _ada(cc, w, b):
    rows, d = cc.shape
    n = w.shape[1]
    tn = next(t for t in (1024, 512, 256, LANES) if n % t == 0)
    return pl.pallas_call(
        _ada_kernel,
        grid=(n // tn,),
        in_specs=[pl.BlockSpec((rows, d), lambda j: (0, 0)),
                  pl.BlockSpec((d, tn), lambda j: (0, j)),
                  pl.BlockSpec((1, tn), lambda j: (0, j))],
        out_specs=pl.BlockSpec((rows, tn), lambda j: (0, j)),
        out_shape=jax.ShapeDtypeStruct((rows, n), F32),
        compiler_params=_params("parallel"),
        name="ada",
    )(cc, w, b.reshape(1, n))


def _rope(t, cos, sin_signed):
    lane = lax.broadcasted_iota(jnp.int32, t.shape, 1)
    partner = jnp.where((lane % 64) < 32, pltpu.roll(t, 96, 1), pltpu.roll(t, 32, 1))
    return t * cos + partner * sin_signed


def _inproj_kernel(x_ref, g1_ref, sh_ref, sc_ref, w_ref, gq_ref, gk_ref, cos_ref, sin_ref,
                   *out_refs, latent):
    h = (_rms(x_ref[0], g1_ref[...]) * (1.0 + sc_ref[0]) + sh_ref[0]).astype(BF16)
    if latent:
        q_ref, k_ref, v_ref, f_ref = out_refs
        cos, sin = cos_ref[...], sin_ref[...]
        q = jnp.dot(h, w_ref[:, 0:ATTN_WIDTH], preferred_element_type=F32)
        scale = HEAD_DIM ** -0.5
        for j in range(N_Q_HEADS):
            sl = slice(j * HEAD_DIM, (j + 1) * HEAD_DIM)
            q_ref[0, :, sl] = (_rope(_rms(q[:, sl], gq_ref[...]), cos, sin) * scale).astype(BF16)
        k0 = ATTN_WIDTH
    else:
        k_ref, v_ref = out_refs
        k0 = 0
    k = jnp.dot(h, w_ref[:, k0:k0 + KV_WIDTH], preferred_element_type=F32)
    for j in range(N_KV_HEADS):
        sl = slice(j * HEAD_DIM, (j + 1) * HEAD_DIM)
        t = _rms(k[:, sl], gk_ref[...])
        if latent:
            t = _rope(t, cos, sin)
        k_ref[0, :, sl] = t.astype(BF16)
    v_ref[0] = jnp.dot(h, w_ref[:, k0 + KV_WIDTH:k0 + 2 * KV_WIDTH],
                       preferred_element_type=F32).astype(BF16)
    if latent:
        f_ref[0] = jnp.dot(h, w_ref[:, ATTN_WIDTH + 2 * KV_WIDTH:],
                           preferred_element_type=F32).astype(BF16)


def _inproj(x, g1, sh, sc, w_bf, gq, gk, cos, sin, latent):
    b, l, d = x.shape
    tm = min(l, 512)
    wcols = w_bf.shape[1]
    fw = wcols - ATTN_WIDTH - 2 * KV_WIDTH
    row = lambda bi, i: (bi, i, 0)
    per_b = pl.BlockSpec((1, 1, d), lambda bi, i: (bi, 0, 0))
    vec = lambda n: pl.BlockSpec((1, n), lambda bi, i: (0, 0))
    if latent:
        w_spec = _resident((d, wcols), lambda bi, i: (0, 0))
        widths = (ATTN_WIDTH, KV_WIDTH, KV_WIDTH, fw)
    else:
        assert ATTN_WIDTH % (2 * KV_WIDTH) == 0
        w_spec = _resident((d, 2 * KV_WIDTH), lambda bi, i: (0, ATTN_WIDTH // (2 * KV_WIDTH)))
        widths = (KV_WIDTH, KV_WIDTH)
    return pl.pallas_call(
        functools.partial(_inproj_kernel, latent=latent),
        grid=(b, l // tm),
        in_specs=[pl.BlockSpec((1, tm, d), row), vec(d), per_b, per_b, w_spec,
                  vec(HEAD_DIM), vec(HEAD_DIM),
                  pl.BlockSpec((tm, HEAD_DIM), lambda bi, i: (i, 0)),
                  pl.BlockSpec((tm, HEAD_DIM), lambda bi, i: (i, 0))],
        out_specs=[pl.BlockSpec((1, tm, n), row) for n in widths],
        out_shape=[jax.ShapeDtypeStruct((b, l, n), BF16) for n in widths],
        compiler_params=_params("parallel", "parallel"),
        name="inproj_latent" if latent else "inproj_ctx",
    )(x, g1, sh, sc, w_bf, gq, gk, cos, sin)


def _attn_kernel(q_ref, k_ref, v_ref, o_ref):
    k = k_ref[0]
    v = v_ref[0]
    for j in range(GQA_GROUP):
        sl = slice(j * HEAD_DIM, (j + 1) * HEAD_DIM)
        s = lax.dot_general(q_ref[0, :, sl], k, NT_DIMS, preferred_element_type=F32)
        p = jnp.exp(s - jnp.max(s, axis=-1, keepdims=True))
        denom = jnp.sum(p, axis=-1, keepdims=True)
        o = jnp.dot(p.astype(BF16), v, preferred_element_type=F32) / denom
        o_ref[0, :, sl] = o.astype(BF16)


def _attention(q, k_all, v_all):
    b, l, _ = q.shape
    lk = k_all.shape[1]
    tq = min(l, 256)
    gw = GQA_GROUP * HEAD_DIM
    return pl.pallas_call(
        _attn_kernel,
        grid=(b, N_KV_HEADS, l // tq),
        in_specs=[pl.BlockSpec((1, tq, gw), lambda bi, g, i: (bi, i, g)),
                  pl.BlockSpec((1, lk, HEAD_DIM), lambda bi, g, i: (bi, 0, g)),
                  pl.BlockSpec((1, lk, HEAD_DIM), lambda bi, g, i: (bi, 0, g))],
        out_specs=pl.BlockSpec((1, tq, gw), lambda bi, g, i: (bi, i, g)),
        out_shape=jax.ShapeDtypeStruct((b, l, ATTN_WIDTH), BF16),
        compiler_params=_params("parallel", "parallel", "parallel"),
        name="attn",
    )(q, k_all, v_all)


def _dft_tables(l, c):
    kl = (np.arange(l, dtype=np.int64)[:, None] * np.arange(l, dtype=np.int64)[None, :]) % l
    ang = 2.0 * np.pi * kl.astype(np.float64) / l
    pos = np.concatenate([np.cos(ang), -np.sin(ang)], axis=1)
    jc = (np.arange(c, dtype=np.int64)[:, None] * np.arange(c, dtype=np.int64)[None, :]) % c
    angc = 2.0 * np.pi * jc.astype(np.float64) / c
    chan = np.stack([np.cos(angc), np.sin(angc)]) / math.sqrt(l * c)

    def split(t):
        head = t.astype(np.float32)
        return jnp.asarray(head) + jnp.asarray((t - head).astype(np.float32))

    return split(pos).astype(BF16), split(chan)


def _fmat_kernel(cs_ref, w_ref, o_ref):
    c = w_ref.shape[-1]
    w = w_ref[0]
    o_ref[0, :, 0:c] = jnp.dot(cs_ref[0], w, preferred_element_type=F32,
                               precision=lax.Precision.HIGHEST).astype(BF16)
    o_ref[0, :, c:2 * c] = jnp.dot(cs_ref[1], w, preferred_element_type=F32,
                                   precision=lax.Precision.HIGHEST).astype(BF16)


def _fmat(chan, w_f):
    g, c, _ = w_f.shape
    return pl.pallas_call(
        _fmat_kernel,
        grid=(g,),
        in_specs=[pl.BlockSpec((2, c, c), lambda gi: (0, 0, 0)),
                  pl.BlockSpec((1, c, c), lambda gi: (gi, 0, 0))],
        out_specs=pl.BlockSpec((1, c, 2 * c), lambda gi: (gi, 0, 0)),
        out_shape=jax.ShapeDtypeStruct((g, c, 2 * c), BF16),
        compiler_params=_params("parallel"),
        name="fmat",
    )(chan, w_f)


def _fourier_kernel(x_ref, m_ref, dft_ref, b_ref, o_ref, z_ref, *, tr):
    l, c = x_ref.shape[1], x_ref.shape[2]
    z = jnp.dot(x_ref[0], m_ref[0], preferred_element_type=F32)
    z_ref[0:l, :] = z[:, 0:c].astype(BF16)
    z_ref[l:2 * l, :] = z[:, c:2 * c].astype(BF16)
    for r in range(l // tr):
        rows = slice(r * tr, (r + 1) * tr)
        y = jnp.dot(dft_ref[rows, :], z_ref[...], preferred_element_type=F32) + b_ref[0]
        o_ref[0, rows, :] = y.astype(BF16)


def _fourier(f, fm, dft, b_f):
    b, l, fw = f.shape
    c = fw // F_GROUPS
    tr = min(l, 512)
    return pl.pallas_call(
        functools.partial(_fourier_kernel, tr=tr),
        grid=(b, F_GROUPS),
        in_specs=[pl.BlockSpec((1, l, c), lambda bi, g: (bi, 0, g)),
                  pl.BlockSpec((1, c, 2 * c), lambda bi, g: (g, 0, 0)),
                  _resident((l, 2 * l), lambda bi, g: (0, 0)),
                  pl.BlockSpec((1, 1, c), lambda bi, g: (g, 0, 0))],
        out_specs=pl.BlockSpec((1, l, c), lambda bi, g: (bi, 0, g)),
        out_shape=jax.ShapeDtypeStruct((b, l, fw), BF16),
        scratch_shapes=[pltpu.VMEM((2 * l, c), BF16)],
        compiler_params=_params("parallel", "parallel"),
        name="fourier",
    )(f, fm, dft, b_f.reshape(F_GROUPS, 1, c))


def _outproj_kernel(a_ref, f_ref, x_ref, wo_ref, gt1_ref, g2_ref, sh2_ref, sc2_ref, wq_ref,
                    xl_ref, h2_ref, qp_ref):
    aw = a_ref.shape[2]
    y = jnp.dot(a_ref[0], wo_ref[0:aw, :], preferred_element_type=F32)
    y = y + jnp.dot(f_ref[0], wo_ref[aw:, :], preferred_element_type=F32)
    xl = x_ref[0] + gt1_ref[0] * y
    xl_ref[0] = xl
    h2 = _rms(xl, g2_ref[...]) * (1.0 + sc2_ref[0]) + sh2_ref[0]
    h2_ref[0] = h2
    qp = jnp.dot(h2.astype(BF16), wq_ref[...], preferred_element_type=F32)
    for j in range(qp_ref.shape[0]):
        qp_ref[j] = qp[:, j * LANES:(j + 1) * LANES].astype(BF16)


def _outproj(attn, four, x, wo_bf, gt1, g2, sh2, sc2, wq_bf):
    b, l, d = x.shape
    tm = min(l, 256)
    nt = l // tm
    aw, fw = attn.shape[2], four.shape[2]
    qw = wq_bf.shape[1]
    row = lambda bi, i: (bi, i, 0)
    per_b = pl.BlockSpec((1, 1, d), lambda bi, i: (bi, 0, 0))
    return pl.pallas_call(
        _outproj_kernel,
        grid=(b, nt),
        in_specs=[pl.BlockSpec((1, tm, aw), row), pl.BlockSpec((1, tm, fw), row),
                  pl.BlockSpec((1, tm, d), row),
                  _resident((aw + fw, d), lambda bi, i: (0, 0)),
                  per_b, pl.BlockSpec((1, d), lambda bi, i: (0, 0)), per_b, per_b,
                  _resident((d, qw), lambda bi, i: (0, 0))],
        out_specs=[pl.BlockSpec((1, tm, d), row), pl.BlockSpec((1, tm, d), row),
                   pl.BlockSpec((qw // LANES, tm, LANES), lambda bi, i: (0, bi * nt + i, 0))],
        out_shape=[jax.ShapeDtypeStruct((b, l, d), F32), jax.ShapeDtypeStruct((b, l, d), F32),
                   jax.ShapeDtypeStruct((qw // LANES, b * l, LANES), BF16)],
        compiler_params=_params("parallel", "parallel"),
        name="outproj",
    )(attn, four, x, wo_bf, gt1, g2, sh2, sc2, wq_bf)


def _topk_rows(s, k, payload=None):
    rows = s.shape[0]
    iota = lax.broadcasted_iota(jnp.int32, s.shape, 0).astype(F32)
    vals, sel = [], []
    for _ in range(k):
        m = jnp.max(s, axis=0, keepdims=True)
        i = jnp.min(jnp.where(s == m, iota, float(rows)), axis=0, keepdims=True)
        hit = iota == i
        vals.append(m)
        sel.append(i if payload is None else
                   jnp.sum(jnp.where(hit, payload, 0.0), axis=0, keepdims=True))
        s = jnp.where(hit, -jnp.inf, s)
    return jnp.concatenate(vals, axis=0), jnp.concatenate(sel, axis=0)


def _pair_candidates(s1, i1, s2, i2, n_keys):
    assert TOPK == 16
    sub = lax.broadcasted_iota(jnp.int32, (8, s1.shape[1]), 0)
    ids = [i1[0:1] * n_keys + i2, i1[1:2] * n_keys + i2[0:8]]
    vals = [s1[0:1] + s2, s1[1:2] + s2[0:8]]
    for a in range(2, 8):
        keep = sub < TOPK // (a + 1)
        vals.append(jnp.where(keep, s1[a:a + 1] + s2[0:8], -jnp.inf))
        ids.append(i1[a:a + 1] * n_keys + i2[0:8])
    vals.append(s1[8:TOPK] + s2[0:1])
    ids.append(i1[8:TOPK] * n_keys + i2[0:1])
    return jnp.concatenate(vals, axis=0), jnp.concatenate(ids, axis=0)


def _retrieve_kernel(q_ref, keys_ref, g_ref, idx_ref, gs_ref, is_ref):
    n_keys = keys_ref.shape[2]

    def head(h, carry):
        halves = []
        for p in range(2):
            s = lax.dot_general(keys_ref[h, p], q_ref[2 * h + p], NT_DIMS,
                                preferred_element_type=F32)
            halves.append(_topk_rows(s, TOPK))
        (s1, i1), (s2, i2) = halves
        cand, cidx = _pair_candidates(s1, i1, s2, i2, float(n_keys))
        sf, eidx = _topk_rows(cand, TOPK, payload=cidx)
        e = jnp.exp(sf - sf[0:1])
        rows = pl.ds(pl.multiple_of(h * TOPK, TOPK), TOPK)
        gs_ref[rows, :] = e / jnp.sum(e, axis=0, keepdims=True)
        is_ref[rows, :] = eidx.astype(jnp.int32)
        return carry

    lax.fori_loop(0, PEER_HEADS, head, 0)
    g_ref[...] = gs_ref[...].T
    idx_ref[0] = is_ref[...]


def _retrieve(qp, keys_bf, tb):
    nq, n, _ = qp.shape
    npick = PEER_HEADS * TOPK
    assert tb == LANES and npick == LANES
    nblk = n // tb
    return pl.pallas_call(
        _retrieve_kernel,
        grid=(nblk,),
        in_specs=[pl.BlockSpec((nq, tb, LANES), lambda i: (0, i, 0)),
                  pl.BlockSpec(keys_bf.shape, lambda i: (0, 0, 0, 0))],
        out_specs=[pl.BlockSpec((tb, npick), lambda i: (i, 0)),
                   pl.BlockSpec((1, npick, tb), lambda i: (i, 0, 0))],
        out_shape=[jax.ShapeDtypeStruct((n, npick), F32),
                   jax.ShapeDtypeStruct((nblk, npick, tb), jnp.int32)],
        scratch_shapes=[pltpu.VMEM((npick, tb), F32), pltpu.VMEM((npick, tb), jnp.int32)],
        compiler_params=_params("parallel"),
        name="retrieve",
    )(qp, keys_bf)


N_SLOTS = 8
BITREV3 = (0, 4, 2, 6, 1, 5, 3, 7)


def _peer_kernel(idx_hbm, uv_hbm, g_ref, gn_ref, h2_ref, h2n_ref, xl_ref, gt2_ref, gf_ref, o_ref,
                 idx_smem, idx_sem, sem, peer_ref, stage_ref, wb0, wb1, part0, part1, *bufs):
    tb, d = h2_ref.shape
    npick = g_ref.shape[1]
    nc = d // LANES
    blk_words = npick * tb
    i = pl.program_id(0)
    nblk = pl.num_programs(0)
    cur = i % 2
    nxt = 1 - cur
    has_next = i + 1 < nblk
    ahead = N_SLOTS - 1

    def idx_copy(blk, half):
        return pltpu.make_async_copy(
            idx_hbm.at[pl.ds(pl.multiple_of(blk * blk_words, blk_words), blk_words)],
            idx_smem.at[pl.ds(pl.multiple_of(half * blk_words, blk_words), blk_words)],
            idx_sem.at[half])

    def slab_copy(word, k, slot):
        return pltpu.make_async_copy(uv_hbm.at[idx_smem[word]], bufs[slot].at[k], sem.at[slot])

    def wait(slot):
        pltpu.make_async_copy(uv_hbm.at[pl.ds(0, npick)], bufs[slot], sem.at[slot]).wait()

    sub = lax.broadcasted_iota(jnp.int32, (8, LANES), 0)

    def fold(a, b, dist):
        m = (sub & dist) == 0
        return (jnp.where(m, a, b)
                + jnp.where(m, pltpu.roll(a, 8 - dist, 0), pltpu.roll(b, dist, 0)))

    def u_side(xrow, slot, part_ref, start_copy=None):
        x = jnp.concatenate([xrow[:, c * LANES:(c + 1) * LANES] for c in range(nc)], axis=0)

        def partial(k):
            p = bufs[slot][k, 0:nc, :].astype(F32) * x
            acc = p[0:8]
            for r in range(1, nc // 8):
                acc = acc + p[8 * r:8 * r + 8]
            return acc

        for grp in range(npick // 8):
            if start_copy is not None:
                for j in range(4):
                    start_copy(grp * 4 + j)
            ps = [partial(grp * 8 + BITREV3[j]) for j in range(8)]
            q = [fold(ps[2 * j], ps[2 * j + 1], 4) for j in range(4)]
            part_ref[grp * 8:grp * 8 + 8, :] = fold(fold(q[0], q[1], 2), fold(q[2], q[3], 2), 1)

    def weights(grow, part_ref, wb_ref):
        a = jnp.sum(part_ref[...].T, axis=0, keepdims=True)
        gelu = 0.5 * a * (1.0 + lax.erf(a * math.sqrt(0.5)))
        wb_ref[...] = jnp.broadcast_to(grow * gelu, (npick, npick)).T

    def v_side(u, slot, wb_ref, start_copy):
        accs = [None] * 4
        for k in range(npick):
            if k % 2 == 0:
                start_copy(npick // 2 + k // 2)
            term = wb_ref[k:k + 1, :] * bufs[slot][k, nc:2 * nc, :].astype(F32)
            accs[k % 4] = term if accs[k % 4] is None else accs[k % 4] + term
        out = (accs[0] + accs[1]) + (accs[2] + accs[3])

        for c in range(nc):
            stage_ref[u:u + 1, c * LANES:(c + 1) * LANES] = out[c:c + 1, :]

    parts = (part0, part1)
    wbs = (wb0, wb1)

    @pl.when(i == 0)
    def _():
        first = idx_copy(0, 0)
        first.start()
        first.wait()
        for t in range(ahead):
            for k in range(npick):
                slab_copy(k * tb + t, k, t).start()
        wait(0)
        u_side(h2_ref[0:1, :], 0, parts[0])

    @pl.when(has_next)
    def _():
        idx_copy(i + 1, nxt).start()

    next_base = jnp.where(has_next, nxt, cur) * blk_words - tb

    last = tb // N_SLOTS - 1

    def row_of(ref, next_ref, r, u, reach):
        row = ref[pl.ds(jnp.minimum(r, tb - 1), 1), :]
        over = u + reach - N_SLOTS
        if over >= 0:
            row = jnp.where(r < tb, row, next_ref[over:over + 1, :])
        return row

    def group_of_tokens(j, carry):
        @pl.when(jnp.logical_and(j == last, has_next))
        def _():
            idx_copy(i + 1, nxt).wait()

        for u in range(N_SLOTS):
            t = j * N_SLOTS + u
            ta = t + ahead
            word0 = jnp.where(ta < tb, cur * blk_words, next_base) + ta
            fill = (u + ahead) % N_SLOTS

            def start_copy(k, word0=word0, fill=fill):
                slab_copy(word0 + k * tb, k, fill).start()

            wait((u + 1) % N_SLOTS)
            weights(g_ref[pl.ds(t, 1), :], parts[u % 2], wbs[u % 2])
            u_side(row_of(h2_ref, h2n_ref, t + 1, u, 1), (u + 1) % N_SLOTS, parts[(u + 1) % 2],
                   start_copy)
            v_side(u, u, wbs[u % 2], start_copy)
        peer_ref[pl.ds(pl.multiple_of(j * N_SLOTS, N_SLOTS), N_SLOTS), :] = stage_ref[...]
        return carry

    lax.fori_loop(0, last + 1, group_of_tokens, 0)

    @pl.when(jnp.logical_not(has_next))
    def _():
        for s in range(1, ahead):
            wait(s)

    y = xl_ref[...] + gt2_ref[0] * peer_ref[...]
    o_ref[...] = _rms(y, gf_ref[...])


def _peer(idx_t, uv, g, h2, xl, gt2, gf, blocks_per_batch):
    nblk, npick, tb = idx_t.shape
    n, d = h2.shape
    nc = d // LANES
    assert tb % N_SLOTS == 0 and N_SLOTS % 8 == 0 and npick % 8 == 0 and nc % 16 == 0
    assert uv.shape[1:] == (2 * nc, LANES) and uv.dtype == BF16
    row = lambda i: (i, 0)
    next_rows = lambda i: (jnp.minimum(i + 1, nblk - 1) * (tb // 8), 0)
    return pl.pallas_call(
        _peer_kernel,
        grid=(nblk,),
        in_specs=[pl.BlockSpec(memory_space=pl.ANY), pl.BlockSpec(memory_space=pl.ANY),
                  pl.BlockSpec((tb, npick), row), pl.BlockSpec((8, npick), next_rows),
                  pl.BlockSpec((tb, d), row), pl.BlockSpec((8, d), next_rows),
                  pl.BlockSpec((tb, d), row),
                  pl.BlockSpec((1, 1, d), lambda i: (i // blocks_per_batch, 0, 0)),
                  pl.BlockSpec((1, d), lambda i: (0, 0))],
        out_specs=pl.BlockSpec((tb, d), row),
        out_shape=jax.ShapeDtypeStruct((n, d), F32),
        scratch_shapes=[pltpu.SMEM((2 * npick * tb,), jnp.int32), pltpu.SemaphoreType.DMA((2,)),
                        pltpu.SemaphoreType.DMA((N_SLOTS,)), pltpu.VMEM((tb, d), F32),
                        pltpu.VMEM((N_SLOTS, d), F32),
                        pltpu.VMEM((npick, npick), F32), pltpu.VMEM((npick, npick), F32),
                        pltpu.VMEM((npick, LANES), F32), pltpu.VMEM((npick, LANES), F32)]
        + [pltpu.VMEM((npick, 2 * nc, LANES), BF16) for _ in range(N_SLOTS)],
        compiler_params=_params("arbitrary"),
        name="peer",
    )(idx_t.reshape(-1), uv, g, g, h2, h2, xl, gt2, gf)


def _rope_tables(length):
    rows = length // GRID_W
    row = jnp.broadcast_to(jnp.arange(rows)[:, None], (rows, GRID_W)).reshape(-1)
    col = jnp.broadcast_to(jnp.arange(GRID_W)[None, :], (rows, GRID_W)).reshape(-1)
    inv_freq = ROPE_THETA ** (-jnp.arange(ROPE_FREQS, dtype=F32) / ROPE_FREQS)
    ar = row.astype(F32)[:, None] * inv_freq
    ac = col.astype(F32)[:, None] * inv_freq
    cos = jnp.concatenate([jnp.cos(ar), jnp.cos(ar), jnp.cos(ac), jnp.cos(ac)], axis=-1)
    sin = jnp.concatenate([-jnp.sin(ar), jnp.sin(ar), -jnp.sin(ac), jnp.sin(ac)], axis=-1)
    return cos, sin


def kernel(x, c, ctx, c_ctx, w_ada, b_ada, g_norm1, w_in, g_q, g_k, w_fourier, b_fourier,
           w_out, g_norm2, w_query, sub_keys, u_experts, v_experts, g_final):
    b, l, d = x.shape
    assert w_ada.shape[0] == 1, "single-layer configuration only"
    layer = 0
    tb = LANES

    rows = -(-(b + 1) // 8) * 8
    cc = jnp.concatenate([c, c_ctx[None, :], jnp.zeros((rows - b - 1, d), F32)], axis=0)
    mod = _ada(cc, w_ada[layer], b_ada[layer])
    sh1, sc1, gt1, sh2, sc2, gt2 = [m.reshape(b, 1, d) for m in jnp.split(mod[:b], N_MOD, axis=-1)]
    csh1, csc1 = [jnp.broadcast_to(m.reshape(1, 1, d), (b, 1, d))
                  for m in jnp.split(mod[b], N_MOD, axis=-1)[:2]]

    cos, sin = _rope_tables(l)
    g1 = g_norm1[layer].reshape(1, d)
    gq = g_q[layer].reshape(1, HEAD_DIM)
    gk = g_k[layer].reshape(1, HEAD_DIM)
    w_in_bf = w_in[layer].astype(BF16)
    q, k_l, v_l, f = _inproj(x, g1, sh1, sc1, w_in_bf, gq, gk, cos, sin, latent=True)
    lc = ctx.shape[1]
    k_c, v_c = _inproj(ctx, g1, csh1, csc1, w_in_bf, gq, gk, cos[:lc], sin[:lc], latent=False)

    attn = _attention(q, jnp.concatenate([k_l, k_c], axis=1), jnp.concatenate([v_l, v_c], axis=1))

    dft, chan = _dft_tables(l, f.shape[2] // F_GROUPS)
    four = _fourier(f, _fmat(chan, w_fourier[layer]), dft, b_fourier[layer])

    xl, h2, qp = _outproj(attn, four, x, w_out[layer].astype(BF16), gt1,
                          g_norm2[layer].reshape(1, d), sh2, sc2, w_query[layer].astype(BF16))

    g, idx_t = _retrieve(qp, sub_keys[layer].astype(BF16), tb)
    n_exp = u_experts.shape[1]
    uv = jnp.concatenate([u_experts[layer].reshape(n_exp, d // LANES, LANES),
                          v_experts[layer].reshape(n_exp, d // LANES, LANES)], axis=1).astype(BF16)
    out = _peer(idx_t, uv, g, h2.reshape(b * l, d), xl.reshape(b * l, d), gt2,
                g_final.reshape(1, d), l // tb)
    return out.reshape(b, l, d)
```

```python
import functools
import math

import numpy as np
import jax
import jax.numpy as jnp
from jax import lax
from jax.experimental import pallas as pl
from jax.experimental.pallas import tpu as pltpu

GRID_W = 64
HEAD_DIM = 128
N_Q_HEADS = 8
N_KV_HEADS = 2
GQA_GROUP = N_Q_HEADS // N_KV_HEADS
ATTN_WIDTH = N_Q_HEADS * HEAD_DIM
KV_WIDTH = N_KV_HEADS * HEAD_DIM
ROPE_THETA = 10000.0
ROPE_FREQS = HEAD_DIM // 4
F_GROUPS = 4
PEER_HEADS = 8
TOPK = 16
N_MOD = 6
EPS = 1e-6

LANES = 128
VMEM_LIMIT = 56 * 1024 * 1024

F32 = jnp.float32
BF16 = jnp.bfloat16
NT_DIMS = (((1,), (1,)), ((), ()))


def _params(*sem):
    return pltpu.CompilerParams(dimension_semantics=sem, vmem_limit_bytes=VMEM_LIMIT)


def _resident(shape, index_map):
    return pl.BlockSpec(shape, index_map, pipeline_mode=pl.Buffered(1))


def _rms(x, g):
    return x * lax.rsqrt(jnp.mean(x * x, axis=-1, keepdims=True) + EPS) * g


def _ada_kernel(c_ref, w_ref, b_ref, o_ref):
    c = c_ref[...]
    a = (c * jax.nn.sigmoid(c)).astype(BF16)
    o_ref[...] = jnp.dot(a, w_ref[...].astype(BF16), preferred_element_type=F32) + b_ref[...]


def _ada(cc, w, b):
    rows, d = cc.shape
    n = w.shape[1]
    tn = next(t for t in (1024, 512, 256, LANES) if n % t == 0)
    return pl.pallas_call(
        _ada_kernel,
        grid=(n // tn,),
        in_specs=[pl.BlockSpec((rows, d), lambda j: (0, 0)),
                  pl.BlockSpec((d, tn), lambda j: (0, j)),
                  pl.BlockSpec((1, tn), lambda j: (0, j))],
        out_specs=pl.BlockSpec((rows, tn), lambda j: (0, j)),
        out_shape=jax.ShapeDtypeStruct((rows, n), F32),
        compiler_params=_params("parallel"),
        name="ada",
    )(cc, w, b.reshape(1, n))


def _rope(t, cos, sin_signed):
    lane = lax.broadcasted_iota(jnp.int32, t.shape, 1)
    partner = jnp.where((lane % 64) < 32, pltpu.roll(t, 96, 1), pltpu.roll(t, 32, 1))
    return t * cos + partner * sin_signed


def _inproj_kernel(x_ref, g1_ref, sh_ref, sc_ref, w_ref, gq_ref, gk_ref, cos_ref, sin_ref,
                   *out_refs, latent):
    h = (_rms(x_ref[0], g1_ref[...]) * (1.0 + sc_ref[0]) + sh_ref[0]).astype(BF16)
    if latent:
        q_ref, k_ref, v_ref, f_ref = out_refs
        cos, sin = cos_ref[...], sin_ref[...]
        q = jnp.dot(h, w_ref[:, 0:ATTN_WIDTH], preferred_element_type=F32)
        scale = HEAD_DIM ** -0.5
        for j in range(N_Q_HEADS):
            sl = slice(j * HEAD_DIM, (j + 1) * HEAD_DIM)
            q_ref[0, :, sl] = (_rope(_rms(q[:, sl], gq_ref[...]), cos, sin) * scale).astype(BF16)
        k0 = ATTN_WIDTH
    else:
        k_ref, v_ref = out_refs
        k0 = 0
    k = jnp.dot(h, w_ref[:, k0:k0 + KV_WIDTH], preferred_element_type=F32)
    for j in range(N_KV_HEADS):
        sl = slice(j * HEAD_DIM, (j + 1) * HEAD_DIM)
        t = _rms(k[:, sl], gk_ref[...])
        if latent:
            t = _rope(t, cos, sin)
        k_ref[0, :, sl] = t.astype(BF16)
    v_ref[0] = jnp.dot(h, w_ref[:, k0 + KV_WIDTH:k0 + 2 * KV_WIDTH],
                       preferred_element_type=F32).astype(BF16)
    if latent:
        f_ref[0] = jnp.dot(h, w_ref[:, ATTN_WIDTH + 2 * KV_WIDTH:],
                           preferred_element_type=F32).astype(BF16)


def _inproj(x, g1, sh, sc, w_bf, gq, gk, cos, sin, latent):
    b, l, d = x.shape
    tm = min(l, 512)
    wcols = w_bf.shape[1]
    fw = wcols - ATTN_WIDTH - 2 * KV_WIDTH
    row = lambda bi, i: (bi, i, 0)
    per_b = pl.BlockSpec((1, 1, d), lambda bi, i: (bi, 0, 0))
    vec = lambda n: pl.BlockSpec((1, n), lambda bi, i: (0, 0))
    if latent:
        w_spec = _resident((d, wcols), lambda bi, i: (0, 0))
        widths = (ATTN_WIDTH, KV_WIDTH, KV_WIDTH, fw)
    else:
        assert ATTN_WIDTH % (2 * KV_WIDTH) == 0
        w_spec = _resident((d, 2 * KV_WIDTH), lambda bi, i: (0, ATTN_WIDTH // (2 * KV_WIDTH)))
        widths = (KV_WIDTH, KV_WIDTH)
    return pl.pallas_call(
        functools.partial(_inproj_kernel, latent=latent),
        grid=(b, l // tm),
        in_specs=[pl.BlockSpec((1, tm, d), row), vec(d), per_b, per_b, w_spec,
                  vec(HEAD_DIM), vec(HEAD_DIM),
                  pl.BlockSpec((tm, HEAD_DIM), lambda bi, i: (i, 0)),
                  pl.BlockSpec((tm, HEAD_DIM), lambda bi, i: (i, 0))],
        out_specs=[pl.BlockSpec((1, tm, n), row) for n in widths],
        out_shape=[jax.ShapeDtypeStruct((b, l, n), BF16) for n in widths],
        compiler_params=_params("parallel", "parallel"),
        name="inproj_latent" if latent else "inproj_ctx",
    )(x, g1, sh, sc, w_bf, gq, gk, cos, sin)


def _attn_kernel(q_ref, k_ref, v_ref, o_ref):
    k = k_ref[0]
    v = v_ref[0]
    for j in range(GQA_GROUP):
        sl = slice(j * HEAD_DIM, (j + 1) * HEAD_DIM)
        s = lax.dot_general(q_ref[0, :, sl], k, NT_DIMS, preferred_element_type=F32)
        p = jnp.exp(s - jnp.max(s, axis=-1, keepdims=True))
        denom = jnp.sum(p, axis=-1, keepdims=True)
        o = jnp.dot(p.astype(BF16), v, preferred_element_type=F32) / denom
        o_ref[0, :, sl] = o.astype(BF16)


def _attention(q, k_all, v_all):
    b, l, _ = q.shape
    lk = k_all.shape[1]
    tq = min(l, 256)
    gw = GQA_GROUP * HEAD_DIM
    return pl.pallas_call(
        _attn_kernel,
        grid=(b, N_KV_HEADS, l // tq),
        in_specs=[pl.BlockSpec((1, tq, gw), lambda bi, g, i: (bi, i, g)),
                  pl.BlockSpec((1, lk, HEAD_DIM), lambda bi, g, i: (bi, 0, g)),
                  pl.BlockSpec((1, lk, HEAD_DIM), lambda bi, g, i: (bi, 0, g))],
        out_specs=pl.BlockSpec((1, tq, gw), lambda bi, g, i: (bi, i, g)),
        out_shape=jax.ShapeDtypeStruct((b, l, ATTN_WIDTH), BF16),
        compiler_params=_params("parallel", "parallel", "parallel"),
        name="attn",
    )(q, k_all, v_all)


def _dft_tables(l, c):
    kl = (np.arange(l, dtype=np.int64)[:, None] * np.arange(l, dtype=np.int64)[None, :]) % l
    ang = 2.0 * np.pi * kl.astype(np.float64) / l
    pos = np.concatenate([np.cos(ang), -np.sin(ang)], axis=1)
    jc = (np.arange(c, dtype=np.int64)[:, None] * np.arange(c, dtype=np.int64)[None, :]) % c
    angc = 2.0 * np.pi * jc.astype(np.float64) / c
    chan = np.stack([np.cos(angc), np.sin(angc)]) / math.sqrt(l * c)

    def split(t):
        head = t.astype(np.float32)
        return jnp.asarray(head) + jnp.asarray((t - head).astype(np.float32))

    return split(pos).astype(BF16), split(chan)


def _fmat_kernel(cs_ref, w_ref, o_ref):
    c = w_ref.shape[-1]
    w = w_ref[0]
    o_ref[0, :, 0:c] = jnp.dot(cs_ref[0], w, preferred_element_type=F32,
                               precision=lax.Precision.HIGHEST).astype(BF16)
    o_ref[0, :, c:2 * c] = jnp.dot(cs_ref[1], w, preferred_element_type=F32,
                                   precision=lax.Precision.HIGHEST).astype(BF16)


def _fmat(chan, w_f):
    g, c, _ = w_f.shape
    return pl.pallas_call(
        _fmat_kernel,
        grid=(g,),
        in_specs=[pl.BlockSpec((2, c, c), lambda gi: (0, 0, 0)),
                  pl.BlockSpec((1, c, c), lambda gi: (gi, 0, 0))],
        out_specs=pl.BlockSpec((1, c, 2 * c), lambda gi: (gi, 0, 0)),
        out_shape=jax.ShapeDtypeStruct((g, c, 2 * c), BF16),
        compiler_params=_params("parallel"),
        name="fmat",
    )(chan, w_f)


def _fourier_kernel(x_ref, m_ref, dft_ref, b_ref, o_ref, z_ref, *, tr):
    l, c = x_ref.shape[1], x_ref.shape[2]
    z = jnp.dot(x_ref[0], m_ref[0], preferred_element_type=F32)
    z_ref[0:l, :] = z[:, 0:c].astype(BF16)
    z_ref[l:2 * l, :] = z[:, c:2 * c].astype(BF16)
    for r in range(l // tr):
        rows = slice(r * tr, (r + 1) * tr)
        y = jnp.dot(dft_ref[rows, :], z_ref[...], preferred_element_type=F32) + b_ref[0]
        o_ref[0, rows, :] = y.astype(BF16)


def _fourier(f, fm, dft, b_f):
    b, l, fw = f.shape
    c = fw // F_GROUPS
    tr = min(l, 512)
    return pl.pallas_call(
        functools.partial(_fourier_kernel, tr=tr),
        grid=(b, F_GROUPS),
        in_specs=[pl.BlockSpec((1, l, c), lambda bi, g: (bi, 0, g)),
                  pl.BlockSpec((1, c, 2 * c), lambda bi, g: (g, 0, 0)),
                  _resident((l, 2 * l), lambda bi, g: (0, 0)),
                  pl.BlockSpec((1, 1, c), lambda bi, g: (g, 0, 0))],
        out_specs=pl.BlockSpec((1, l, c), lambda bi, g: (bi, 0, g)),
        out_shape=jax.ShapeDtypeStruct((b, l, fw), BF16),
        scratch_shapes=[pltpu.VMEM((2 * l, c), BF16)],
        compiler_params=_params("parallel", "parallel"),
        name="fourier",
    )(f, fm, dft, b_f.reshape(F_GROUPS, 1, c))


def _outproj_kernel(a_ref, f_ref, x_ref, wo_ref, gt1_ref, g2_ref, sh2_ref, sc2_ref, wq_ref,
                    xl_ref, h2_ref, qp_ref):
    aw = a_ref.shape[2]
    y = jnp.dot(a_ref[0], wo_ref[0:aw, :], preferred_element_type=F32)
    y = y + jnp.dot(f_ref[0], wo_ref[aw:, :], preferred_element_type=F32)
    xl = x_ref[0] + gt1_ref[0] * y
    xl_ref[0] = xl
    h2 = _rms(xl, g2_ref[...]) * (1.0 + sc2_ref[0]) + sh2_ref[0]
    h2_ref[0] = h2
    qp = jnp.dot(h2.astype(BF16), wq_ref[...], preferred_element_type=F32)
    for j in range(qp_ref.shape[0]):
        qp_ref[j] = qp[:, j * LANES:(j + 1) * LANES].astype(BF16)


def _outproj(attn, four, x, wo_bf, gt1, g2, sh2, sc2, wq_bf):
    b, l, d = x.shape
    tm = min(l, 256)
    nt = l // tm
    aw, fw = attn.shape[2], four.shape[2]
    qw = wq_bf.shape[1]
    row = lambda bi, i: (bi, i, 0)
    per_b = pl.BlockSpec((1, 1, d), lambda bi, i: (bi, 0, 0))
    return pl.pallas_call(
        _outproj_kernel,
        grid=(b, nt),
        in_specs=[pl.BlockSpec((1, tm, aw), row), pl.BlockSpec((1, tm, fw), row),
                  pl.BlockSpec((1, tm, d), row),
                  _resident((aw + fw, d), lambda bi, i: (0, 0)),
                  per_b, pl.BlockSpec((1, d), lambda bi, i: (0, 0)), per_b, per_b,
                  _resident((d, qw), lambda bi, i: (0, 0))],
        out_specs=[pl.BlockSpec((1, tm, d), row), pl.BlockSpec((1, tm, d), row),
                   pl.BlockSpec((qw // LANES, tm, LANES), lambda bi, i: (0, bi * nt + i, 0))],
        out_shape=[jax.ShapeDtypeStruct((b, l, d), F32), jax.ShapeDtypeStruct((b, l, d), F32),
                   jax.ShapeDtypeStruct((qw // LANES, b * l, LANES), BF16)],
        compiler_params=_params("parallel", "parallel"),
        name="outproj",
    )(attn, four, x, wo_bf, gt1, g2, sh2, sc2, wq_bf)


def _topk_rows(s, k, payload=None):
    rows = s.shape[0]
    iota = lax.broadcasted_iota(jnp.int32, s.shape, 0).astype(F32)
    vals, sel = [], []
    for _ in range(k):
        m = jnp.max(s, axis=0, keepdims=True)
        i = jnp.min(jnp.where(s == m, iota, float(rows)), axis=0, keepdims=True)
        hit = iota == i
        vals.append(m)
        sel.append(i if payload is None else
                   jnp.sum(jnp.where(hit, payload, 0.0), axis=0, keepdims=True))
        s = jnp.where(hit, -jnp.inf, s)
    return jnp.concatenate(vals, axis=0), jnp.concatenate(sel, axis=0)


def _pair_candidates(s1, i1, s2, i2, n_keys):
    assert TOPK == 16
    sub = lax.broadcasted_iota(jnp.int32, (8, s1.shape[1]), 0)
    ids = [i1[0:1] * n_keys + i2, i1[1:2] * n_keys + i2[0:8]]
    vals = [s1[0:1] + s2, s1[1:2] + s2[0:8]]
    for a in range(2, 8):
        keep = sub < TOPK // (a + 1)
        vals.append(jnp.where(keep, s1[a:a + 1] + s2[0:8], -jnp.inf))
        ids.append(i1[a:a + 1] * n_keys + i2[0:8])
    vals.append(s1[8:TOPK] + s2[0:1])
    ids.append(i1[8:TOPK] * n_keys + i2[0:1])
    return jnp.concatenate(vals, axis=0), jnp.concatenate(ids, axis=0)


def _retrieve_kernel(q_ref, keys_ref, g_ref, idx_ref, gs_ref, is_ref):
    n_keys = keys_ref.shape[2]

    def head(h, carry):
        halves = []
        for p in range(2):
            s = lax.dot_general(keys_ref[h, p], q_ref[2 * h + p], NT_DIMS,
                                preferred_element_type=F32)
            halves.append(_topk_rows(s, TOPK))
        (s1, i1), (s2, i2) = halves
        cand, cidx = _pair_candidates(s1, i1, s2, i2, float(n_keys))
        sf, eidx = _topk_rows(cand, TOPK, payload=cidx)
        e = jnp.exp(sf - sf[0:1])
        rows = pl.ds(pl.multiple_of(h * TOPK, TOPK), TOPK)
        gs_ref[rows, :] = e / jnp.sum(e, axis=0, keepdims=True)
        is_ref[rows, :] = eidx.astype(jnp.int32)
        return carry

    lax.fori_loop(0, PEER_HEADS, head, 0)
    g_ref[...] = gs_ref[...].T
    idx_ref[0] = is_ref[...]


def _retrieve(qp, keys_bf, tb):
    nq, n, _ = qp.shape
    npick = PEER_HEADS * TOPK
    assert tb == LANES and npick == LANES
    nblk = n // tb
    return pl.pallas_call(
        _retrieve_kernel,
        grid=(nblk,),
        in_specs=[pl.BlockSpec((nq, tb, LANES), lambda i: (0, i, 0)),
                  pl.BlockSpec(keys_bf.shape, lambda i: (0, 0, 0, 0))],
        out_specs=[pl.BlockSpec((tb, npick), lambda i: (i, 0)),
                   pl.BlockSpec((1, npick, tb), lambda i: (i, 0, 0))],
        out_shape=[jax.ShapeDtypeStruct((n, npick), F32),
                   jax.ShapeDtypeStruct((nblk, npick, tb), jnp.int32)],
        scratch_shapes=[pltpu.VMEM((npick, tb), F32), pltpu.VMEM((npick, tb), jnp.int32)],
        compiler_params=_params("parallel"),
        name="retrieve",
    )(qp, keys_bf)


N_SLOTS = 8
BITREV3 = (0, 4, 2, 6, 1, 5, 3, 7)
DMA_THREADS = 2


def _peer_kernel(idx_hbm, uv_hbm, g_ref, gn_ref, h2_ref, h2n_ref, xl_ref, gt2_ref, gf_ref, o_ref,
                 idx_smem, idx_sem, sem, peer_ref, stage_ref, wb0, wb1, part0, part1, *bufs):
    tb, d = h2_ref.shape
    npick = g_ref.shape[1]
    nc = d // LANES
    blk_words = npick * tb
    i = pl.program_id(0)
    nblk = pl.num_programs(0)
    cur = i % 2
    nxt = 1 - cur
    has_next = i + 1 < nblk
    ahead = N_SLOTS - 1

    def idx_copy(blk, half):
        return pltpu.make_async_copy(
            idx_hbm.at[pl.ds(pl.multiple_of(blk * blk_words, blk_words), blk_words)],
            idx_smem.at[pl.ds(pl.multiple_of(half * blk_words, blk_words), blk_words)],
            idx_sem.at[half])

    def slab_copy(word, k, slot):
        return pltpu.make_async_copy(uv_hbm.at[idx_smem[word]], bufs[slot].at[k], sem.at[slot])

    def wait(slot):
        pltpu.make_async_copy(uv_hbm.at[pl.ds(0, npick)], bufs[slot], sem.at[slot]).wait()

    sub = lax.broadcasted_iota(jnp.int32, (8, LANES), 0)

    def fold(a, b, dist):
        m = (sub & dist) == 0
        return (jnp.where(m, a, b)
                + jnp.where(m, pltpu.roll(a, 8 - dist, 0), pltpu.roll(b, dist, 0)))

    def u_side(xrow, slot, part_ref, start_copy=None):
        x = jnp.concatenate([xrow[:, c * LANES:(c + 1) * LANES] for c in range(nc)], axis=0)

        def partial(k):
            p = bufs[slot][k, 0:nc, :].astype(F32) * x
            acc = p[0:8]
            for r in range(1, nc // 8):
                acc = acc + p[8 * r:8 * r + 8]
            return acc

        for grp in range(npick // 8):
            if start_copy is not None:
                for j in range(4):
                    start_copy(grp * 4 + j)
            ps = [partial(grp * 8 + BITREV3[j]) for j in range(8)]
            q = [fold(ps[2 * j], ps[2 * j + 1], 4) for j in range(4)]
            part_ref[grp * 8:grp * 8 + 8, :] = fold(fold(q[0], q[1], 2), fold(q[2], q[3], 2), 1)

    def weights(grow, part_ref, wb_ref):
        a = jnp.sum(part_ref[...].T, axis=0, keepdims=True)
        gelu = 0.5 * a * (1.0 + lax.erf(a * math.sqrt(0.5)))
        wb_ref[...] = jnp.broadcast_to(grow * gelu, (npick, npick)).T

    def v_side(u, slot, wb_ref, start_copy):
        accs = [None] * 4
        for k in range(npick):
            if k % 2 == 0:
                start_copy(npick // 2 + k // 2)
            term = wb_ref[k:k + 1, :] * bufs[slot][k, nc:2 * nc, :].astype(F32)
            accs[k % 4] = term if accs[k % 4] is None else accs[k % 4] + term
        out = (accs[0] + accs[1]) + (accs[2] + accs[3])

        for c in range(nc):
            stage_ref[u:u + 1, c * LANES:(c + 1) * LANES] = out[c:c + 1, :]

    parts = (part0, part1)
    wbs = (wb0, wb1)

    @pl.when(i == 0)
    def _():
        first = idx_copy(0, 0)
        first.start()
        first.wait()
        for t in range(ahead):
            for k in range(npick):
                slab_copy(k * tb + t, k, t).start()
        wait(0)
        u_side(h2_ref[0:1, :], 0, parts[0])

    @pl.when(has_next)
    def _():
        idx_copy(i + 1, nxt).start()

    next_base = jnp.where(has_next, nxt, cur) * blk_words - tb

    last = tb // N_SLOTS - 1

    def row_of(ref, next_ref, r, u, reach):
        row = ref[pl.ds(jnp.minimum(r, tb - 1), 1), :]
        over = u + reach - N_SLOTS
        if over >= 0:
            row = jnp.where(r < tb, row, next_ref[over:over + 1, :])
        return row

    def group_of_tokens(j, carry):
        @pl.when(jnp.logical_and(j == last, has_next))
        def _():
            idx_copy(i + 1, nxt).wait()

        for u in range(N_SLOTS):
            t = j * N_SLOTS + u
            ta = t + ahead
            word0 = jnp.where(ta < tb, cur * blk_words, next_base) + ta
            fill = (u + ahead) % N_SLOTS

            def start_copy(k, word0=word0, fill=fill):
                slab_copy(word0 + k * tb, k, fill).start(priority=k % DMA_THREADS)

            wait((u + 1) % N_SLOTS)
            weights(g_ref[pl.ds(t, 1), :], parts[u % 2], wbs[u % 2])
            u_side(row_of(h2_ref, h2n_ref, t + 1, u, 1), (u + 1) % N_SLOTS, parts[(u + 1) % 2],
                   start_copy)
            v_side(u, u, wbs[u % 2], start_copy)
        peer_ref[pl.ds(pl.multiple_of(j * N_SLOTS, N_SLOTS), N_SLOTS), :] = stage_ref[...]
        return carry

    lax.fori_loop(0, last + 1, group_of_tokens, 0)

    @pl.when(jnp.logical_not(has_next))
    def _():
        for s in range(1, ahead):
            wait(s)

    y = xl_ref[...] + gt2_ref[0] * peer_ref[...]
    o_ref[...] = _rms(y, gf_ref[...])


def _peer(idx_t, uv, g, h2, xl, gt2, gf, blocks_per_batch):
    nblk, npick, tb = idx_t.shape
    n, d = h2.shape
    nc = d // LANES
    assert tb % N_SLOTS == 0 and N_SLOTS % 8 == 0 and npick % 8 == 0 and nc % 16 == 0
    assert uv.shape[1:] == (2 * nc, LANES) and uv.dtype == BF16
    row = lambda i: (i, 0)
    next_rows = lambda i: (jnp.minimum(i + 1, nblk - 1) * (tb // 8), 0)
    return pl.pallas_call(
        _peer_kernel,
        grid=(nblk,),
        in_specs=[pl.BlockSpec(memory_space=pl.ANY), pl.BlockSpec(memory_space=pl.ANY),
                  pl.BlockSpec((tb, npick), row), pl.BlockSpec((8, npick), next_rows),
                  pl.BlockSpec((tb, d), row), pl.BlockSpec((8, d), next_rows),
                  pl.BlockSpec((tb, d), row),
                  pl.BlockSpec((1, 1, d), lambda i: (i // blocks_per_batch, 0, 0)),
                  pl.BlockSpec((1, d), lambda i: (0, 0))],
        out_specs=pl.BlockSpec((tb, d), row),
        out_shape=jax.ShapeDtypeStruct((n, d), F32),
        scratch_shapes=[pltpu.SMEM((2 * npick * tb,), jnp.int32), pltpu.SemaphoreType.DMA((2,)),
                        pltpu.SemaphoreType.DMA((N_SLOTS,)), pltpu.VMEM((tb, d), F32),
                        pltpu.VMEM((N_SLOTS, d), F32),
                        pltpu.VMEM((npick, npick), F32), pltpu.VMEM((npick, npick), F32),
                        pltpu.VMEM((npick, LANES), F32), pltpu.VMEM((npick, LANES), F32)]
        + [pltpu.VMEM((npick, 2 * nc, LANES), BF16) for _ in range(N_SLOTS)],
        compiler_params=_params("arbitrary"),
        name="peer",
    )(idx_t.reshape(-1), uv, g, g, h2, h2, xl, gt2, gf)


def _rope_tables(length):
    rows = length // GRID_W
    row = jnp.broadcast_to(jnp.arange(rows)[:, None], (rows, GRID_W)).reshape(-1)
    col = jnp.broadcast_to(jnp.arange(GRID_W)[None, :], (rows, GRID_W)).reshape(-1)
    inv_freq = ROPE_THETA ** (-jnp.arange(ROPE_FREQS, dtype=F32) / ROPE_FREQS)
    ar = row.astype(F32)[:, None] * inv_freq
    ac = col.astype(F32)[:, None] * inv_freq
    cos = jnp.concatenate([jnp.cos(ar), jnp.cos(ar), jnp.cos(ac), jnp.cos(ac)], axis=-1)
    sin = jnp.concatenate([-jnp.sin(ar), jnp.sin(ar), -jnp.sin(ac), jnp.sin(ac)], axis=-1)
    return cos, sin


def kernel(x, c, ctx, c_ctx, w_ada, b_ada, g_norm1, w_in, g_q, g_k, w_fourier, b_fourier,
           w_out, g_norm2, w_query, sub_keys, u_experts, v_experts, g_final):
    b, l, d = x.shape
    assert w_ada.shape[0] == 1, "single-layer configuration only"
    layer = 0
    tb = LANES

    rows = -(-(b + 1) // 8) * 8
    cc = jnp.concatenate([c, c_ctx[None, :], jnp.zeros((rows - b - 1, d), F32)], axis=0)
    mod = _ada(cc, w_ada[layer], b_ada[layer])
    sh1, sc1, gt1, sh2, sc2, gt2 = [m.reshape(b, 1, d) for m in jnp.split(mod[:b], N_MOD, axis=-1)]
    csh1, csc1 = [jnp.broadcast_to(m.reshape(1, 1, d), (b, 1, d))
                  for m in jnp.split(mod[b], N_MOD, axis=-1)[:2]]

    cos, sin = _rope_tables(l)
    g1 = g_norm1[layer].reshape(1, d)
    gq = g_q[layer].reshape(1, HEAD_DIM)
    gk = g_k[layer].reshape(1, HEAD_DIM)
    w_in_bf = w_in[layer].astype(BF16)
    q, k_l, v_l, f = _inproj(x, g1, sh1, sc1, w_in_bf, gq, gk, cos, sin, latent=True)
    lc = ctx.shape[1]
    k_c, v_c = _inproj(ctx, g1, csh1, csc1, w_in_bf, gq, gk, cos[:lc], sin[:lc], latent=False)

    attn = _attention(q, jnp.concatenate([k_l, k_c], axis=1), jnp.concatenate([v_l, v_c], axis=1))

    dft, chan = _dft_tables(l, f.shape[2] // F_GROUPS)
    four = _fourier(f, _fmat(chan, w_fourier[layer]), dft, b_fourier[layer])

    xl, h2, qp = _outproj(attn, four, x, w_out[layer].astype(BF16), gt1,
                          g_norm2[layer].reshape(1, d), sh2, sc2, w_query[layer].astype(BF16))

    g, idx_t = _retrieve(qp, sub_keys[layer].astype(BF16), tb)
    n_exp = u_experts.shape[1]
    uv = jnp.concatenate([u_experts[layer].reshape(n_exp, d // LANES, LANES),
                          v_experts[layer].reshape(n_exp, d // LANES, LANES)], axis=1).astype(BF16)
    out = _peer(idx_t, uv, g, h2.reshape(b * l, d), xl.reshape(b * l, d), gt2,
                g_final.reshape(1, d), l // tb)
    return out.reshape(b, l, d)
```

```python
import functools
import math

import numpy as np
import jax
import jax.numpy as jnp
from jax import lax
from jax.experimental import pallas as pl
from jax.experimental.pallas import tpu as pltpu

GRID_W = 64
HEAD_DIM = 128
N_Q_HEADS = 8
N_KV_HEADS = 2
GQA_GROUP = N_Q_HEADS // N_KV_HEADS
ATTN_WIDTH = N_Q_HEADS * HEAD_DIM
KV_WIDTH = N_KV_HEADS * HEAD_DIM
ROPE_THETA = 10000.0
ROPE_FREQS = HEAD_DIM // 4
F_GROUPS = 4
PEER_HEADS = 8
TOPK = 16
N_MOD = 6
EPS = 1e-6

LANES = 128
VMEM_LIMIT = 56 * 1024 * 1024

F32 = jnp.float32
BF16 = jnp.bfloat16
NT_DIMS = (((1,), (1,)), ((), ()))


def _params(*sem):
    return pltpu.CompilerParams(dimension_semantics=sem, vmem_limit_bytes=VMEM_LIMIT)


def _resident(shape, index_map):
    return pl.BlockSpec(shape, index_map, pipeline_mode=pl.Buffered(1))


def _rms(x, g):
    return x * lax.rsqrt(jnp.mean(x * x, axis=-1, keepdims=True) + EPS) * g


def _ada_kernel(c_ref, w_ref, b_ref, o_ref):
    c = c_ref[...]
    a = (c * jax.nn.sigmoid(c)).astype(BF16)
    o_ref[...] = jnp.dot(a, w_ref[...].astype(BF16), preferred_element_type=F32) + b_ref[...]


def _ada(cc, w, b):
    rows, d = cc.shape
    n = w.shape[1]
    tn = next(t for t in (1024, 512, 256, LANES) if n % t == 0)
    return pl.pallas_call(
        _ada_kernel,
        grid=(n // tn,),
        in_specs=[pl.BlockSpec((rows, d), lambda j: (0, 0)),
                  pl.BlockSpec((d, tn), lambda j: (0, j)),
                  pl.BlockSpec((1, tn), lambda j: (0, j))],
        out_specs=pl.BlockSpec((rows, tn), lambda j: (0, j)),
        out_shape=jax.ShapeDtypeStruct((rows, n), F32),
        compiler_params=_params("parallel"),
        name="ada",
    )(cc, w, b.reshape(1, n))


def _rope(t, cos, sin_signed):
    lane = lax.broadcasted_iota(jnp.int32, t.shape, 1)
    partner = jnp.where((lane % 64) < 32, pltpu.roll(t, 96, 1), pltpu.roll(t, 32, 1))
    return t * cos + partner * sin_signed


def _inproj_kernel(x_ref, g1_ref, sh_ref, sc_ref, w_ref, gq_ref, gk_ref, cos_ref, sin_ref,
                   *out_refs, latent):
    h = (_rms(x_ref[0], g1_ref[...]) * (1.0 + sc_ref[0]) + sh_ref[0]).astype(BF16)
    if latent:
        q_ref, k_ref, v_ref, f_ref = out_refs
        cos, sin = cos_ref[...], sin_ref[...]
        q = jnp.dot(h, w_ref[:, 0:ATTN_WIDTH], preferred_element_type=F32)
        scale = HEAD_DIM ** -0.5
        for j in range(N_Q_HEADS):
            sl = slice(j * HEAD_DIM, (j + 1) * HEAD_DIM)
            q_ref[0, :, sl] = (_rope(_rms(q[:, sl], gq_ref[...]), cos, sin) * scale).astype(BF16)
        k0 = ATTN_WIDTH
    else:
        k_ref, v_ref = out_refs
        k0 = 0
    k = jnp.dot(h, w_ref[:, k0:k0 + KV_WIDTH], preferred_element_type=F32)
    for j in range(N_KV_HEADS):
        sl = slice(j * HEAD_DIM, (j + 1) * HEAD_DIM)
        t = _rms(k[:, sl], gk_ref[...])
        if latent:
            t = _rope(t, cos, sin)
        k_ref[0, :, sl] = t.astype(BF16)
    v_ref[0] = jnp.dot(h, w_ref[:, k0 + KV_WIDTH:k0 + 2 * KV_WIDTH],
                       preferred_element_type=F32).astype(BF16)
    if latent:
        f_ref[0] = jnp.dot(h, w_ref[:, ATTN_WIDTH + 2 * KV_WIDTH:],
                           preferred_element_type=F32).astype(BF16)


def _inproj(x, g1, sh, sc, w_bf, gq, gk, cos, sin, latent):
    b, l, d = x.shape
    tm = min(l, 512)
    wcols = w_bf.shape[1]
    fw = wcols - ATTN_WIDTH - 2 * KV_WIDTH
    row = lambda bi, i: (bi, i, 0)
    per_b = pl.BlockSpec((1, 1, d), lambda bi, i: (bi, 0, 0))
    vec = lambda n: pl.BlockSpec((1, n), lambda bi, i: (0, 0))
    if latent:
        w_spec = _resident((d, wcols), lambda bi, i: (0, 0))
        widths = (ATTN_WIDTH, KV_WIDTH, KV_WIDTH, fw)
    else:
        assert ATTN_WIDTH % (2 * KV_WIDTH) == 0
        w_spec = _resident((d, 2 * KV_WIDTH), lambda bi, i: (0, ATTN_WIDTH // (2 * KV_WIDTH)))
        widths = (KV_WIDTH, KV_WIDTH)
    return pl.pallas_call(
        functools.partial(_inproj_kernel, latent=latent),
        grid=(b, l // tm),
        in_specs=[pl.BlockSpec((1, tm, d), row), vec(d), per_b, per_b, w_spec,
                  vec(HEAD_DIM), vec(HEAD_DIM),
                  pl.BlockSpec((tm, HEAD_DIM), lambda bi, i: (i, 0)),
                  pl.BlockSpec((tm, HEAD_DIM), lambda bi, i: (i, 0))],
        out_specs=[pl.BlockSpec((1, tm, n), row) for n in widths],
        out_shape=[jax.ShapeDtypeStruct((b, l, n), BF16) for n in widths],
        compiler_params=_params("parallel", "parallel"),
        name="inproj_latent" if latent else "inproj_ctx",
    )(x, g1, sh, sc, w_bf, gq, gk, cos, sin)


def _attn_kernel(q_ref, k_ref, v_ref, o_ref):
    k = k_ref[0]
    v = v_ref[0]
    for j in range(GQA_GROUP):
        sl = slice(j * HEAD_DIM, (j + 1) * HEAD_DIM)
        s = lax.dot_general(q_ref[0, :, sl], k, NT_DIMS, preferred_element_type=F32)
        p = jnp.exp(s - jnp.max(s, axis=-1, keepdims=True))
        denom = jnp.sum(p, axis=-1, keepdims=True)
        o = jnp.dot(p.astype(BF16), v, preferred_element_type=F32) / denom
        o_ref[0, :, sl] = o.astype(BF16)


def _attention(q, k_all, v_all):
    b, l, _ = q.shape
    lk = k_all.shape[1]
    tq = min(l, 256)
    gw = GQA_GROUP * HEAD_DIM
    return pl.pallas_call(
        _attn_kernel,
        grid=(b, N_KV_HEADS, l // tq),
        in_specs=[pl.BlockSpec((1, tq, gw), lambda bi, g, i: (bi, i, g)),
                  pl.BlockSpec((1, lk, HEAD_DIM), lambda bi, g, i: (bi, 0, g)),
                  pl.BlockSpec((1, lk, HEAD_DIM), lambda bi, g, i: (bi, 0, g))],
        out_specs=pl.BlockSpec((1, tq, gw), lambda bi, g, i: (bi, i, g)),
        out_shape=jax.ShapeDtypeStruct((b, l, ATTN_WIDTH), BF16),
        compiler_params=_params("parallel", "parallel", "parallel"),
        name="attn",
    )(q, k_all, v_all)


def _dft_tables(l, c):
    kl = (np.arange(l, dtype=np.int64)[:, None] * np.arange(l, dtype=np.int64)[None, :]) % l
    ang = 2.0 * np.pi * kl.astype(np.float64) / l
    pos = np.concatenate([np.cos(ang), -np.sin(ang)], axis=1)
    jc = (np.arange(c, dtype=np.int64)[:, None] * np.arange(c, dtype=np.int64)[None, :]) % c
    angc = 2.0 * np.pi * jc.astype(np.float64) / c
    chan = np.stack([np.cos(angc), np.sin(angc)]) / math.sqrt(l * c)

    def split(t):
        head = t.astype(np.float32)
        return jnp.asarray(head) + jnp.asarray((t - head).astype(np.float32))

    return split(pos).astype(BF16), split(chan)


def _fmat_kernel(cs_ref, w_ref, o_ref):
    c = w_ref.shape[-1]
    w = w_ref[0]
    o_ref[0, :, 0:c] = jnp.dot(cs_ref[0], w, preferred_element_type=F32,
                               precision=lax.Precision.HIGHEST).astype(BF16)
    o_ref[0, :, c:2 * c] = jnp.dot(cs_ref[1], w, preferred_element_type=F32,
                                   precision=lax.Precision.HIGHEST).astype(BF16)


def _fmat(chan, w_f):
    g, c, _ = w_f.shape
    return pl.pallas_call(
        _fmat_kernel,
        grid=(g,),
        in_specs=[pl.BlockSpec((2, c, c), lambda gi: (0, 0, 0)),
                  pl.BlockSpec((1, c, c), lambda gi: (gi, 0, 0))],
        out_specs=pl.BlockSpec((1, c, 2 * c), lambda gi: (gi, 0, 0)),
        out_shape=jax.ShapeDtypeStruct((g, c, 2 * c), BF16),
        compiler_params=_params("parallel"),
        name="fmat",
    )(chan, w_f)


def _fourier_kernel(x_ref, m_ref, dft_ref, b_ref, o_ref, z_ref, *, tr):
    l, c = x_ref.shape[1], x_ref.shape[2]
    z = jnp.dot(x_ref[0], m_ref[0], preferred_element_type=F32)
    z_ref[0:l, :] = z[:, 0:c].astype(BF16)
    z_ref[l:2 * l, :] = z[:, c:2 * c].astype(BF16)
    for r in range(l // tr):
        rows = slice(r * tr, (r + 1) * tr)
        y = jnp.dot(dft_ref[rows, :], z_ref[...], preferred_element_type=F32) + b_ref[0]
        o_ref[0, rows, :] = y.astype(BF16)


def _fourier(f, fm, dft, b_f):
    b, l, fw = f.shape
    c = fw // F_GROUPS
    tr = min(l, 512)
    return pl.pallas_call(
        functools.partial(_fourier_kernel, tr=tr),
        grid=(b, F_GROUPS),
        in_specs=[pl.BlockSpec((1, l, c), lambda bi, g: (bi, 0, g)),
                  pl.BlockSpec((1, c, 2 * c), lambda bi, g: (g, 0, 0)),
                  _resident((l, 2 * l), lambda bi, g: (0, 0)),
                  pl.BlockSpec((1, 1, c), lambda bi, g: (g, 0, 0))],
        out_specs=pl.BlockSpec((1, l, c), lambda bi, g: (bi, 0, g)),
        out_shape=jax.ShapeDtypeStruct((b, l, fw), BF16),
        scratch_shapes=[pltpu.VMEM((2 * l, c), BF16)],
        compiler_params=_params("parallel", "parallel"),
        name="fourier",
    )(f, fm, dft, b_f.reshape(F_GROUPS, 1, c))


def _outproj_kernel(a_ref, f_ref, x_ref, wo_ref, gt1_ref, g2_ref, sh2_ref, sc2_ref, wq_ref,
                    xl_ref, h2_ref, qp_ref):
    aw = a_ref.shape[2]
    y = jnp.dot(a_ref[0], wo_ref[0:aw, :], preferred_element_type=F32)
    y = y + jnp.dot(f_ref[0], wo_ref[aw:, :], preferred_element_type=F32)
    xl = x_ref[0] + gt1_ref[0] * y
    xl_ref[0] = xl
    h2 = _rms(xl, g2_ref[...]) * (1.0 + sc2_ref[0]) + sh2_ref[0]
    h2_ref[0] = h2
    qp = jnp.dot(h2.astype(BF16), wq_ref[...], preferred_element_type=F32)
    for j in range(qp_ref.shape[0]):
        qp_ref[j] = qp[:, j * LANES:(j + 1) * LANES].astype(BF16)


def _outproj(attn, four, x, wo_bf, gt1, g2, sh2, sc2, wq_bf):
    b, l, d = x.shape
    tm = min(l, 256)
    nt = l // tm
    aw, fw = attn.shape[2], four.shape[2]
    qw = wq_bf.shape[1]
    row = lambda bi, i: (bi, i, 0)
    per_b = pl.BlockSpec((1, 1, d), lambda bi, i: (bi, 0, 0))
    return pl.pallas_call(
        _outproj_kernel,
        grid=(b, nt),
        in_specs=[pl.BlockSpec((1, tm, aw), row), pl.BlockSpec((1, tm, fw), row),
                  pl.BlockSpec((1, tm, d), row),
                  _resident((aw + fw, d), lambda bi, i: (0, 0)),
                  per_b, pl.BlockSpec((1, d), lambda bi, i: (0, 0)), per_b, per_b,
                  _resident((d, qw), lambda bi, i: (0, 0))],
        out_specs=[pl.BlockSpec((1, tm, d), row), pl.BlockSpec((1, tm, d), row),
                   pl.BlockSpec((qw // LANES, tm, LANES), lambda bi, i: (0, bi * nt + i, 0))],
        out_shape=[jax.ShapeDtypeStruct((b, l, d), F32), jax.ShapeDtypeStruct((b, l, d), F32),
                   jax.ShapeDtypeStruct((qw // LANES, b * l, LANES), BF16)],
        compiler_params=_params("parallel", "parallel"),
        name="outproj",
    )(attn, four, x, wo_bf, gt1, g2, sh2, sc2, wq_bf)


def _topk_rows(s, k, payload=None):
    rows = s.shape[0]
    iota = lax.broadcasted_iota(jnp.int32, s.shape, 0).astype(F32)
    vals, sel = [], []
    for _ in range(k):
        m = jnp.max(s, axis=0, keepdims=True)
        i = jnp.min(jnp.where(s == m, iota, float(rows)), axis=0, keepdims=True)
        hit = iota == i
        vals.append(m)
        sel.append(i if payload is None else
                   jnp.sum(jnp.where(hit, payload, 0.0), axis=0, keepdims=True))
        s = jnp.where(hit, -jnp.inf, s)
    return jnp.concatenate(vals, axis=0), jnp.concatenate(sel, axis=0)


def _pair_candidates(s1, i1, s2, i2, n_keys):
    assert TOPK == 16
    sub = lax.broadcasted_iota(jnp.int32, (8, s1.shape[1]), 0)
    ids = [i1[0:1] * n_keys + i2, i1[1:2] * n_keys + i2[0:8]]
    vals = [s1[0:1] + s2, s1[1:2] + s2[0:8]]
    for a in range(2, 8):
        keep = sub < TOPK // (a + 1)
        vals.append(jnp.where(keep, s1[a:a + 1] + s2[0:8], -jnp.inf))
        ids.append(i1[a:a + 1] * n_keys + i2[0:8])
    vals.append(s1[8:TOPK] + s2[0:1])
    ids.append(i1[8:TOPK] * n_keys + i2[0:1])
    return jnp.concatenate(vals, axis=0), jnp.concatenate(ids, axis=0)


def _retrieve_kernel(q_ref, keys_ref, g_ref, idx_ref, gs_ref, is_ref):
    n_keys = keys_ref.shape[2]

    def head(h, carry):
        halves = []
        for p in range(2):
            s = lax.dot_general(keys_ref[h, p], q_ref[2 * h + p], NT_DIMS,
                                preferred_element_type=F32)
            halves.append(_topk_rows(s, TOPK))
        (s1, i1), (s2, i2) = halves
        cand, cidx = _pair_candidates(s1, i1, s2, i2, float(n_keys))
        sf, eidx = _topk_rows(cand, TOPK, payload=cidx)
        e = jnp.exp(sf - sf[0:1])
        rows = pl.ds(pl.multiple_of(h * TOPK, TOPK), TOPK)
        gs_ref[rows, :] = e / jnp.sum(e, axis=0, keepdims=True)
        is_ref[rows, :] = eidx.astype(jnp.int32)
        return carry

    lax.fori_loop(0, PEER_HEADS, head, 0)
    g_ref[...] = gs_ref[...].T
    idx_ref[0] = is_ref[...]


def _retrieve(qp, keys_bf, tb):
    nq, n, _ = qp.shape
    npick = PEER_HEADS * TOPK
    assert tb == LANES and npick == LANES
    nblk = n // tb
    return pl.pallas_call(
        _retrieve_kernel,
        grid=(nblk,),
        in_specs=[pl.BlockSpec((nq, tb, LANES), lambda i: (0, i, 0)),
                  pl.BlockSpec(keys_bf.shape, lambda i: (0, 0, 0, 0))],
        out_specs=[pl.BlockSpec((tb, npick), lambda i: (i, 0)),
                   pl.BlockSpec((1, npick, tb), lambda i: (i, 0, 0))],
        out_shape=[jax.ShapeDtypeStruct((n, npick), F32),
                   jax.ShapeDtypeStruct((nblk, npick, tb), jnp.int32)],
        scratch_shapes=[pltpu.VMEM((npick, tb), F32), pltpu.VMEM((npick, tb), jnp.int32)],
        compiler_params=_params("parallel"),
        name="retrieve",
    )(qp, keys_bf)


N_SLOTS = 8
DMA_THREADS = 2


def _peer_kernel(idx_hbm, uv_hbm, sel_ref, g_ref, gn_ref, h2_ref, h2n_ref, xl_ref, gt2_ref, gf_ref,
                 o_ref, idx_smem, idx_sem, sem, peer_ref, stage_ref, p2_ref, wb0, wb1, part0, part1,
                 *bufs):
    tb, d = h2_ref.shape
    npick = g_ref.shape[1]
    nc = d // LANES
    blk_words = npick * tb
    i = pl.program_id(0)
    nblk = pl.num_programs(0)
    cur = i % 2
    nxt = 1 - cur
    has_next = i + 1 < nblk
    ahead = N_SLOTS - 1

    def idx_copy(blk, half):
        return pltpu.make_async_copy(
            idx_hbm.at[pl.ds(pl.multiple_of(blk * blk_words, blk_words), blk_words)],
            idx_smem.at[pl.ds(pl.multiple_of(half * blk_words, blk_words), blk_words)],
            idx_sem.at[half])

    def slab_copy(word, k, slot):
        return pltpu.make_async_copy(uv_hbm.at[idx_smem[word]], bufs[slot].at[k], sem.at[slot])

    def wait(slot):
        pltpu.make_async_copy(uv_hbm.at[pl.ds(0, npick)], bufs[slot], sem.at[slot]).wait()

    def u_side(xrow, slot, part_ref, start_copy=None):
        x = jnp.concatenate([xrow[:, c * LANES:(c + 1) * LANES] for c in range(nc)], axis=0)

        def partial(k):
            p = bufs[slot][k, 0:nc, :].astype(F32) * x
            acc = p[0:8]
            for r in range(1, nc // 8):
                acc = acc + p[8 * r:8 * r + 8]
            return acc

        for grp in range(npick // 8):
            if start_copy is not None:
                for j in range(4):
                    start_copy(grp * 4 + j)
            ps = jnp.concatenate([partial(grp * 8 + j) for j in range(8)], axis=0)
            p2_ref[grp * 64:grp * 64 + 64, :] = ps.astype(BF16)
        part_ref[...] = jnp.dot(sel_ref[...], p2_ref[...], preferred_element_type=F32)

    def weights(grow, part_ref, wb_ref):
        a = jnp.sum(part_ref[...].T, axis=0, keepdims=True)
        gelu = 0.5 * a * (1.0 + lax.erf(a * math.sqrt(0.5)))
        wb_ref[...] = jnp.broadcast_to(grow * gelu, (npick, npick)).T

    def v_side(u, slot, wb_ref, start_copy):
        accs = [None] * 4
        for k in range(npick):
            if k % 2 == 0:
                start_copy(npick // 2 + k // 2)
            term = wb_ref[k:k + 1, :] * bufs[slot][k, nc:2 * nc, :].astype(F32)
            accs[k % 4] = term if accs[k % 4] is None else accs[k % 4] + term
        out = (accs[0] + accs[1]) + (accs[2] + accs[3])

        for c in range(nc):
            stage_ref[u:u + 1, c * LANES:(c + 1) * LANES] = out[c:c + 1, :]

    parts = (part0, part1)
    wbs = (wb0, wb1)

    @pl.when(i == 0)
    def _():
        first = idx_copy(0, 0)
        first.start()
        first.wait()
        for t in range(ahead):
            for k in range(npick):
                slab_copy(k * tb + t, k, t).start()
        wait(0)
        u_side(h2_ref[0:1, :], 0, parts[0])

    @pl.when(has_next)
    def _():
        idx_copy(i + 1, nxt).start()

    next_base = jnp.where(has_next, nxt, cur) * blk_words - tb

    last = tb // N_SLOTS - 1

    def row_of(ref, next_ref, r, u, reach):
        row = ref[pl.ds(jnp.minimum(r, tb - 1), 1), :]
        over = u + reach - N_SLOTS
        if over >= 0:
            row = jnp.where(r < tb, row, next_ref[over:over + 1, :])
        return row

    def group_of_tokens(j, carry):
        @pl.when(jnp.logical_and(j == last, has_next))
        def _():
            idx_copy(i + 1, nxt).wait()

        for u in range(N_SLOTS):
            t = j * N_SLOTS + u
            ta = t + ahead
            word0 = jnp.where(ta < tb, cur * blk_words, next_base) + ta
            fill = (u + ahead) % N_SLOTS

            def start_copy(k, word0=word0, fill=fill):
                slab_copy(word0 + k * tb, k, fill).start(priority=k % DMA_THREADS)

            wait((u + 1) % N_SLOTS)
            weights(g_ref[pl.ds(t, 1), :], parts[u % 2], wbs[u % 2])
            u_side(row_of(h2_ref, h2n_ref, t + 1, u, 1), (u + 1) % N_SLOTS, parts[(u + 1) % 2],
                   start_copy)
            v_side(u, u, wbs[u % 2], start_copy)
        peer_ref[pl.ds(pl.multiple_of(j * N_SLOTS, N_SLOTS), N_SLOTS), :] = stage_ref[...]
        return carry

    lax.fori_loop(0, last + 1, group_of_tokens, 0)

    @pl.when(jnp.logical_not(has_next))
    def _():
        for s in range(1, ahead):
            wait(s)

    y = xl_ref[...] + gt2_ref[0] * peer_ref[...]
    o_ref[...] = _rms(y, gf_ref[...])


def _peer(idx_t, uv, g, h2, xl, gt2, gf, blocks_per_batch):
    nblk, npick, tb = idx_t.shape
    n, d = h2.shape
    nc = d // LANES
    assert tb % N_SLOTS == 0 and N_SLOTS % 8 == 0 and npick % 8 == 0 and nc % 16 == 0
    assert uv.shape[1:] == (2 * nc, LANES) and uv.dtype == BF16
    sel = jnp.asarray(np.kron(np.eye(npick), np.ones((1, 8))), dtype=BF16)
    row = lambda i: (i, 0)
    next_rows = lambda i: (jnp.minimum(i + 1, nblk - 1) * (tb // 8), 0)
    return pl.pallas_call(
        _peer_kernel,
        grid=(nblk,),
        in_specs=[pl.BlockSpec(memory_space=pl.ANY), pl.BlockSpec(memory_space=pl.ANY),
                  pl.BlockSpec((npick, 8 * npick), lambda i: (0, 0)),
                  pl.BlockSpec((tb, npick), row), pl.BlockSpec((8, npick), next_rows),
                  pl.BlockSpec((tb, d), row), pl.BlockSpec((8, d), next_rows),
                  pl.BlockSpec((tb, d), row),
                  pl.BlockSpec((1, 1, d), lambda i: (i // blocks_per_batch, 0, 0)),
                  pl.BlockSpec((1, d), lambda i: (0, 0))],
        out_specs=pl.BlockSpec((tb, d), row),
        out_shape=jax.ShapeDtypeStruct((n, d), F32),
        scratch_shapes=[pltpu.SMEM((2 * npick * tb,), jnp.int32), pltpu.SemaphoreType.DMA((2,)),
                        pltpu.SemaphoreType.DMA((N_SLOTS,)), pltpu.VMEM((tb, d), F32),
                        pltpu.VMEM((N_SLOTS, d), F32), pltpu.VMEM((8 * npick, LANES), BF16),
                        pltpu.VMEM((npick, npick), F32), pltpu.VMEM((npick, npick), F32),
                        pltpu.VMEM((npick, LANES), F32), pltpu.VMEM((npick, LANES), F32)]
        + [pltpu.VMEM((npick, 2 * nc, LANES), BF16) for _ in range(N_SLOTS)],
        compiler_params=_params("arbitrary"),
        name="peer",
    )(idx_t.reshape(-1), uv, sel, g, g, h2, h2, xl, gt2, gf)


def _rope_tables(length):
    rows = length // GRID_W
    row = jnp.broadcast_to(jnp.arange(rows)[:, None], (rows, GRID_W)).reshape(-1)
    col = jnp.broadcast_to(jnp.arange(GRID_W)[None, :], (rows, GRID_W)).reshape(-1)
    inv_freq = ROPE_THETA ** (-jnp.arange(ROPE_FREQS, dtype=F32) / ROPE_FREQS)
    ar = row.astype(F32)[:, None] * inv_freq
    ac = col.astype(F32)[:, None] * inv_freq
    cos = jnp.concatenate([jnp.cos(ar), jnp.cos(ar), jnp.cos(ac), jnp.cos(ac)], axis=-1)
    sin = jnp.concatenate([-jnp.sin(ar), jnp.sin(ar), -jnp.sin(ac), jnp.sin(ac)], axis=-1)
    return cos, sin


def kernel(x, c, ctx, c_ctx, w_ada, b_ada, g_norm1, w_in, g_q, g_k, w_fourier, b_fourier,
           w_out, g_norm2, w_query, sub_keys, u_experts, v_experts, g_final):
    b, l, d = x.shape
    assert w_ada.shape[0] == 1, "single-layer configuration only"
    layer = 0
    tb = LANES

    rows = -(-(b + 1) // 8) * 8
    cc = jnp.concatenate([c, c_ctx[None, :], jnp.zeros((rows - b - 1, d), F32)], axis=0)
    mod = _ada(cc, w_ada[layer], b_ada[layer])
    sh1, sc1, gt1, sh2, sc2, gt2 = [m.reshape(b, 1, d) for m in jnp.split(mod[:b], N_MOD, axis=-1)]
    csh1, csc1 = [jnp.broadcast_to(m.reshape(1, 1, d), (b, 1, d))
                  for m in jnp.split(mod[b], N_MOD, axis=-1)[:2]]

    cos, sin = _rope_tables(l)
    g1 = g_norm1[layer].reshape(1, d)
    gq = g_q[layer].reshape(1, HEAD_DIM)
    gk = g_k[layer].reshape(1, HEAD_DIM)
    w_in_bf = w_in[layer].astype(BF16)
    q, k_l, v_l, f = _inproj(x, g1, sh1, sc1, w_in_bf, gq, gk, cos, sin, latent=True)
    lc = ctx.shape[1]
    k_c, v_c = _inproj(ctx, g1, csh1, csc1, w_in_bf, gq, gk, cos[:lc], sin[:lc], latent=False)

    attn = _attention(q, jnp.concatenate([k_l, k_c], axis=1), jnp.concatenate([v_l, v_c], axis=1))

    dft, chan = _dft_tables(l, f.shape[2] // F_GROUPS)
    four = _fourier(f, _fmat(chan, w_fourier[layer]), dft, b_fourier[layer])

    xl, h2, qp = _outproj(attn, four, x, w_out[layer].astype(BF16), gt1,
                          g_norm2[layer].reshape(1, d), sh2, sc2, w_query[layer].astype(BF16))

    g, idx_t = _retrieve(qp, sub_keys[layer].astype(BF16), tb)
    n_exp = u_experts.shape[1]
    uv = jnp.concatenate([u_experts[layer].reshape(n_exp, d // LANES, LANES),
                          v_experts[layer].reshape(n_exp, d // LANES, LANES)], axis=1).astype(BF16)
    out = _peer(idx_t, uv, g, h2.reshape(b * l, d), xl.reshape(b * l, d), gt2,
                g_final.reshape(1, d), l // tb)
    return out.reshape(b, l, d)
```

```python
import functools
import math

import numpy as np
import jax
import jax.numpy as jnp
from jax import lax
from jax.experimental import pallas as pl
from jax.experimental.pallas import tpu as pltpu

GRID_W = 64
HEAD_DIM = 128
N_Q_HEADS = 8
N_KV_HEADS = 2
GQA_GROUP = N_Q_HEADS // N_KV_HEADS
ATTN_WIDTH = N_Q_HEADS * HEAD_DIM
KV_WIDTH = N_KV_HEADS * HEAD_DIM
ROPE_THETA = 10000.0
ROPE_FREQS = HEAD_DIM // 4
F_GROUPS = 4
PEER_HEADS = 8
TOPK = 16
N_MOD = 6
EPS = 1e-6

LANES = 128
VMEM_LIMIT = 56 * 1024 * 1024

F32 = jnp.float32
BF16 = jnp.bfloat16
NT_DIMS = (((1,), (1,)), ((), ()))


def _params(*sem):
    return pltpu.CompilerParams(dimension_semantics=sem, vmem_limit_bytes=VMEM_LIMIT)


def _resident(shape, index_map):
    return pl.BlockSpec(shape, index_map, pipeline_mode=pl.Buffered(1))


def _rms(x, g):
    return x * lax.rsqrt(jnp.mean(x * x, axis=-1, keepdims=True) + EPS) * g


def _ada_kernel(c_ref, w_ref, b_ref, o_ref):
    c = c_ref[...]
    a = (c * jax.nn.sigmoid(c)).astype(BF16)
    o_ref[...] = jnp.dot(a, w_ref[...].astype(BF16), preferred_element_type=F32) + b_ref[...]


def _ada(cc, w, b):
    rows, d = cc.shape
    n = w.shape[1]
    tn = next(t for t in (1024, 512, 256, LANES) if n % t == 0)
    return pl.pallas_call(
        _ada_kernel,
        grid=(n // tn,),
        in_specs=[pl.BlockSpec((rows, d), lambda j: (0, 0)),
                  pl.BlockSpec((d, tn), lambda j: (0, j)),
                  pl.BlockSpec((1, tn), lambda j: (0, j))],
        out_specs=pl.BlockSpec((rows, tn), lambda j: (0, j)),
        out_shape=jax.ShapeDtypeStruct((rows, n), F32),
        compiler_params=_params("parallel"),
        name="ada",
    )(cc, w, b.reshape(1, n))


def _rope(t, cos, sin_signed):
    lane = lax.broadcasted_iota(jnp.int32, t.shape, 1)
    partner = jnp.where((lane % 64) < 32, pltpu.roll(t, 96, 1), pltpu.roll(t, 32, 1))
    return t * cos + partner * sin_signed


def _inproj_kernel(x_ref, g1_ref, sh_ref, sc_ref, w_ref, gq_ref, gk_ref, cos_ref, sin_ref,
                   *out_refs, latent):
    h = (_rms(x_ref[0], g1_ref[...]) * (1.0 + sc_ref[0]) + sh_ref[0]).astype(BF16)
    if latent:
        q_ref, k_ref, v_ref, f_ref = out_refs
        cos, sin = cos_ref[...], sin_ref[...]
        q = jnp.dot(h, w_ref[:, 0:ATTN_WIDTH], preferred_element_type=F32)
        scale = HEAD_DIM ** -0.5
        for j in range(N_Q_HEADS):
            sl = slice(j * HEAD_DIM, (j + 1) * HEAD_DIM)
            q_ref[0, :, sl] = (_rope(_rms(q[:, sl], gq_ref[...]), cos, sin) * scale).astype(BF16)
        k0 = ATTN_WIDTH
    else:
        k_ref, v_ref = out_refs
        k0 = 0
    k = jnp.dot(h, w_ref[:, k0:k0 + KV_WIDTH], preferred_element_type=F32)
    for j in range(N_KV_HEADS):
        sl = slice(j * HEAD_DIM, (j + 1) * HEAD_DIM)
        t = _rms(k[:, sl], gk_ref[...])
        if latent:
            t = _rope(t, cos, sin)
        k_ref[0, :, sl] = t.astype(BF16)
    v_ref[0] = jnp.dot(h, w_ref[:, k0 + KV_WIDTH:k0 + 2 * KV_WIDTH],
                       preferred_element_type=F32).astype(BF16)
    if latent:
        f_ref[0] = jnp.dot(h, w_ref[:, ATTN_WIDTH + 2 * KV_WIDTH:],
                           preferred_element_type=F32).astype(BF16)


def _inproj(x, g1, sh, sc, w_bf, gq, gk, cos, sin, latent):
    b, l, d = x.shape
    tm = min(l, 512)
    wcols = w_bf.shape[1]
    fw = wcols - ATTN_WIDTH - 2 * KV_WIDTH
    row = lambda bi, i: (bi, i, 0)
    per_b = pl.BlockSpec((1, 1, d), lambda bi, i: (bi, 0, 0))
    vec = lambda n: pl.BlockSpec((1, n), lambda bi, i: (0, 0))
    if latent:
        w_spec = _resident((d, wcols), lambda bi, i: (0, 0))
        widths = (ATTN_WIDTH, KV_WIDTH, KV_WIDTH, fw)
    else:
        assert ATTN_WIDTH % (2 * KV_WIDTH) == 0
        w_spec = _resident((d, 2 * KV_WIDTH), lambda bi, i: (0, ATTN_WIDTH // (2 * KV_WIDTH)))
        widths = (KV_WIDTH, KV_WIDTH)
    return pl.pallas_call(
        functools.partial(_inproj_kernel, latent=latent),
        grid=(b, l // tm),
        in_specs=[pl.BlockSpec((1, tm, d), row), vec(d), per_b, per_b, w_spec,
                  vec(HEAD_DIM), vec(HEAD_DIM),
                  pl.BlockSpec((tm, HEAD_DIM), lambda bi, i: (i, 0)),
                  pl.BlockSpec((tm, HEAD_DIM), lambda bi, i: (i, 0))],
        out_specs=[pl.BlockSpec((1, tm, n), row) for n in widths],
        out_shape=[jax.ShapeDtypeStruct((b, l, n), BF16) for n in widths],
        compiler_params=_params("parallel", "parallel"),
        name="inproj_latent" if latent else "inproj_ctx",
    )(x, g1, sh, sc, w_bf, gq, gk, cos, sin)


def _attn_kernel(q_ref, k_ref, v_ref, o_ref):
    k = k_ref[0]
    v = v_ref[0]
    for j in range(GQA_GROUP):
        sl = slice(j * HEAD_DIM, (j + 1) * HEAD_DIM)
        s = lax.dot_general(q_ref[0, :, sl], k, NT_DIMS, preferred_element_type=F32)
        p = jnp.exp(s - jnp.max(s, axis=-1, keepdims=True))
        denom = jnp.sum(p, axis=-1, keepdims=True)
        o = jnp.dot(p.astype(BF16), v, preferred_element_type=F32) / denom
        o_ref[0, :, sl] = o.astype(BF16)


def _attention(q, k_all, v_all):
    b, l, _ = q.shape
    lk = k_all.shape[1]
    tq = min(l, 256)
    gw = GQA_GROUP * HEAD_DIM
    return pl.pallas_call(
        _attn_kernel,
        grid=(b, N_KV_HEADS, l // tq),
        in_specs=[pl.BlockSpec((1, tq, gw), lambda bi, g, i: (bi, i, g)),
                  pl.BlockSpec((1, lk, HEAD_DIM), lambda bi, g, i: (bi, 0, g)),
                  pl.BlockSpec((1, lk, HEAD_DIM), lambda bi, g, i: (bi, 0, g))],
        out_specs=pl.BlockSpec((1, tq, gw), lambda bi, g, i: (bi, i, g)),
        out_shape=jax.ShapeDtypeStruct((b, l, ATTN_WIDTH), BF16),
        compiler_params=_params("parallel", "parallel", "parallel"),
        name="attn",
    )(q, k_all, v_all)


def _dft_tables(l, c):
    kl = (np.arange(l, dtype=np.int64)[:, None] * np.arange(l, dtype=np.int64)[None, :]) % l
    ang = 2.0 * np.pi * kl.astype(np.float64) / l
    pos = np.concatenate([np.cos(ang), -np.sin(ang)], axis=1)
    jc = (np.arange(c, dtype=np.int64)[:, None] * np.arange(c, dtype=np.int64)[None, :]) % c
    angc = 2.0 * np.pi * jc.astype(np.float64) / c
    chan = np.stack([np.cos(angc), np.sin(angc)]) / math.sqrt(l * c)

    def split(t):
        head = t.astype(np.float32)
        return jnp.asarray(head) + jnp.asarray((t - head).astype(np.float32))

    return split(pos).astype(BF16), split(chan)


def _fmat_kernel(cs_ref, w_ref, o_ref):
    c = w_ref.shape[-1]
    w = w_ref[0]
    o_ref[0, :, 0:c] = jnp.dot(cs_ref[0], w, preferred_element_type=F32,
                               precision=lax.Precision.HIGHEST).astype(BF16)
    o_ref[0, :, c:2 * c] = jnp.dot(cs_ref[1], w, preferred_element_type=F32,
                                   precision=lax.Precision.HIGHEST).astype(BF16)


def _fmat(chan, w_f):
    g, c, _ = w_f.shape
    return pl.pallas_call(
        _fmat_kernel,
        grid=(g,),
        in_specs=[pl.BlockSpec((2, c, c), lambda gi: (0, 0, 0)),
                  pl.BlockSpec((1, c, c), lambda gi: (gi, 0, 0))],
        out_specs=pl.BlockSpec((1, c, 2 * c), lambda gi: (gi, 0, 0)),
        out_shape=jax.ShapeDtypeStruct((g, c, 2 * c), BF16),
        compiler_params=_params("parallel"),
        name="fmat",
    )(chan, w_f)


def _fourier_kernel(x_ref, m_ref, dft_ref, b_ref, o_ref, z_ref, *, tr):
    l, c = x_ref.shape[1], x_ref.shape[2]
    z = jnp.dot(x_ref[0], m_ref[0], preferred_element_type=F32)
    z_ref[0:l, :] = z[:, 0:c].astype(BF16)
    z_ref[l:2 * l, :] = z[:, c:2 * c].astype(BF16)
    for r in range(l // tr):
        rows = slice(r * tr, (r + 1) * tr)
        y = jnp.dot(dft_ref[rows, :], z_ref[...], preferred_element_type=F32) + b_ref[0]
        o_ref[0, rows, :] = y.astype(BF16)


def _fourier(f, fm, dft, b_f):
    b, l, fw = f.shape
    c = fw // F_GROUPS
    tr = min(l, 512)
    return pl.pallas_call(
        functools.partial(_fourier_kernel, tr=tr),
        grid=(b, F_GROUPS),
        in_specs=[pl.BlockSpec((1, l, c), lambda bi, g: (bi, 0, g)),
                  pl.BlockSpec((1, c, 2 * c), lambda bi, g: (g, 0, 0)),
                  _resident((l, 2 * l), lambda bi, g: (0, 0)),
                  pl.BlockSpec((1, 1, c), lambda bi, g: (g, 0, 0))],
        out_specs=pl.BlockSpec((1, l, c), lambda bi, g: (bi, 0, g)),
        out_shape=jax.ShapeDtypeStruct((b, l, fw), BF16),
        scratch_shapes=[pltpu.VMEM((2 * l, c), BF16)],
        compiler_params=_params("parallel", "parallel"),
        name="fourier",
    )(f, fm, dft, b_f.reshape(F_GROUPS, 1, c))


def _outproj_kernel(a_ref, f_ref, x_ref, wo_ref, gt1_ref, g2_ref, sh2_ref, sc2_ref, wq_ref,
                    xl_ref, h2_ref, qp_ref):
    aw = a_ref.shape[2]
    y = jnp.dot(a_ref[0], wo_ref[0:aw, :], preferred_element_type=F32)
    y = y + jnp.dot(f_ref[0], wo_ref[aw:, :], preferred_element_type=F32)
    xl = x_ref[0] + gt1_ref[0] * y
    xl_ref[0] = xl
    h2 = _rms(xl, g2_ref[...]) * (1.0 + sc2_ref[0]) + sh2_ref[0]
    h2_ref[0] = h2
    qp = jnp.dot(h2.astype(BF16), wq_ref[...], preferred_element_type=F32)
    for j in range(qp_ref.shape[0]):
        qp_ref[j] = qp[:, j * LANES:(j + 1) * LANES].astype(BF16)


def _outproj(attn, four, x, wo_bf, gt1, g2, sh2, sc2, wq_bf):
    b, l, d = x.shape
    tm = min(l, 256)
    nt = l // tm
    aw, fw = attn.shape[2], four.shape[2]
    qw = wq_bf.shape[1]
    row = lambda bi, i: (bi, i, 0)
    per_b = pl.BlockSpec((1, 1, d), lambda bi, i: (bi, 0, 0))
    return pl.pallas_call(
        _outproj_kernel,
        grid=(b, nt),
        in_specs=[pl.BlockSpec((1, tm, aw), row), pl.BlockSpec((1, tm, fw), row),
                  pl.BlockSpec((1, tm, d), row),
                  _resident((aw + fw, d), lambda bi, i: (0, 0)),
                  per_b, pl.BlockSpec((1, d), lambda bi, i: (0, 0)), per_b, per_b,
                  _resident((d, qw), lambda bi, i: (0, 0))],
        out_specs=[pl.BlockSpec((1, tm, d), row), pl.BlockSpec((1, tm, d), row),
                   pl.BlockSpec((qw // LANES, tm, LANES), lambda bi, i: (0, bi * nt + i, 0))],
        out_shape=[jax.ShapeDtypeStruct((b, l, d), F32), jax.ShapeDtypeStruct((b, l, d), F32),
                   jax.ShapeDtypeStruct((qw // LANES, b * l, LANES), BF16)],
        compiler_params=_params("parallel", "parallel"),
        name="outproj",
    )(attn, four, x, wo_bf, gt1, g2, sh2, sc2, wq_bf)


def _topk_rows(s, k, payload=None):
    rows = s.shape[0]
    iota = lax.broadcasted_iota(jnp.int32, s.shape, 0).astype(F32)
    vals, sel = [], []
    for _ in range(k):
        m = jnp.max(s, axis=0, keepdims=True)
        i = jnp.min(jnp.where(s == m, iota, float(rows)), axis=0, keepdims=True)
        hit = iota == i
        vals.append(m)
        sel.append(i if payload is None else
                   jnp.sum(jnp.where(hit, payload, 0.0), axis=0, keepdims=True))
        s = jnp.where(hit, -jnp.inf, s)
    return jnp.concatenate(vals, axis=0), jnp.concatenate(sel, axis=0)


def _pair_candidates(s1, i1, s2, i2, n_keys):
    assert TOPK == 16
    sub = lax.broadcasted_iota(jnp.int32, (8, s1.shape[1]), 0)
    ids = [i1[0:1] * n_keys + i2, i1[1:2] * n_keys + i2[0:8]]
    vals = [s1[0:1] + s2, s1[1:2] + s2[0:8]]
    for a in range(2, 8):
        keep = sub < TOPK // (a + 1)
        vals.append(jnp.where(keep, s1[a:a + 1] + s2[0:8], -jnp.inf))
        ids.append(i1[a:a + 1] * n_keys + i2[0:8])
    vals.append(s1[8:TOPK] + s2[0:1])
    ids.append(i1[8:TOPK] * n_keys + i2[0:1])
    return jnp.concatenate(vals, axis=0), jnp.concatenate(ids, axis=0)


def _retrieve_kernel(q_ref, keys_ref, g_ref, idx_ref, gs_ref, is_ref):
    n_keys = keys_ref.shape[2]

    tokens = q_ref.shape[1]

    def head_pair(hp, carry):
        cands, cidxs = [], []
        for hh in range(2):
            h = 2 * hp + hh
            s = jnp.concatenate(
                [lax.dot_general(keys_ref[h, p], q_ref[2 * h + p], NT_DIMS,
                                 preferred_element_type=F32) for p in range(2)],
                axis=1)
            sv, si = _topk_rows(s, TOPK)
            cand, cidx = _pair_candidates(sv[:, :tokens], si[:, :tokens],
                                          sv[:, tokens:], si[:, tokens:], float(n_keys))
            cands.append(cand)
            cidxs.append(cidx)
        sf, eidx = _topk_rows(jnp.concatenate(cands, axis=1), TOPK,
                              payload=jnp.concatenate(cidxs, axis=1))
        e = jnp.exp(sf - sf[0:1])
        g = e / jnp.sum(e, axis=0, keepdims=True)
        for hh in range(2):
            rows = pl.ds(pl.multiple_of((2 * hp + hh) * TOPK, TOPK), TOPK)
            lanes = slice(hh * tokens, (hh + 1) * tokens)
            gs_ref[rows, :] = g[:, lanes]
            is_ref[rows, :] = eidx[:, lanes].astype(jnp.int32)
        return carry

    assert PEER_HEADS % 2 == 0
    lax.fori_loop(0, PEER_HEADS // 2, head_pair, 0)
    g_ref[...] = gs_ref[...].T
    idx_ref[0] = is_ref[...]


def _retrieve(qp, keys_bf, tb):
    nq, n, _ = qp.shape
    npick = PEER_HEADS * TOPK
    assert tb == LANES and npick == LANES
    nblk = n // tb
    return pl.pallas_call(
        _retrieve_kernel,
        grid=(nblk,),
        in_specs=[pl.BlockSpec((nq, tb, LANES), lambda i: (0, i, 0)),
                  pl.BlockSpec(keys_bf.shape, lambda i: (0, 0, 0, 0))],
        out_specs=[pl.BlockSpec((tb, npick), lambda i: (i, 0)),
                   pl.BlockSpec((1, npick, tb), lambda i: (i, 0, 0))],
        out_shape=[jax.ShapeDtypeStruct((n, npick), F32),
                   jax.ShapeDtypeStruct((nblk, npick, tb), jnp.int32)],
        scratch_shapes=[pltpu.VMEM((npick, tb), F32), pltpu.VMEM((npick, tb), jnp.int32)],
        compiler_params=_params("parallel"),
        name="retrieve",
    )(qp, keys_bf)


N_SLOTS = 8
BITREV3 = (0, 4, 2, 6, 1, 5, 3, 7)
DMA_THREADS = 2


def _peer_kernel(idx_hbm, uv_hbm, g_ref, gn_ref, h2_ref, h2n_ref, xl_ref, gt2_ref, gf_ref, o_ref,
                 idx_smem, idx_sem, sem, peer_ref, stage_ref, wb0, wb1, part0, part1, *bufs):
    tb, d = h2_ref.shape
    npick = g_ref.shape[1]
    nc = d // LANES
    blk_words = npick * tb
    i = pl.program_id(0)
    nblk = pl.num_programs(0)
    cur = i % 2
    nxt = 1 - cur
    has_next = i + 1 < nblk
    ahead = N_SLOTS - 1

    def idx_copy(blk, half):
        return pltpu.make_async_copy(
            idx_hbm.at[pl.ds(pl.multiple_of(blk * blk_words, blk_words), blk_words)],
            idx_smem.at[pl.ds(pl.multiple_of(half * blk_words, blk_words), blk_words)],
            idx_sem.at[half])

    def slab_copy(word, k, slot):
        return pltpu.make_async_copy(uv_hbm.at[idx_smem[word]], bufs[slot].at[k], sem.at[slot])

    def wait(slot):
        pltpu.make_async_copy(uv_hbm.at[pl.ds(0, npick)], bufs[slot], sem.at[slot]).wait()

    sub = lax.broadcasted_iota(jnp.int32, (8, LANES), 0)

    def fold(a, b, dist):
        m = (sub & dist) == 0
        return (jnp.where(m, a, b)
                + jnp.where(m, pltpu.roll(a, 8 - dist, 0), pltpu.roll(b, dist, 0)))

    def u_side(xrow, slot, part_ref, start_copy=None):
        x = jnp.concatenate([xrow[:, c * LANES:(c + 1) * LANES] for c in range(nc)], axis=0)

        def partial(k):
            p = bufs[slot][k, 0:nc, :].astype(F32) * x
            acc = p[0:8]
            for r in range(1, nc // 8):
                acc = acc + p[8 * r:8 * r + 8]
            return acc

        for grp in range(npick // 8):
            if start_copy is not None:
                for j in range(4):
                    start_copy(grp * 4 + j)
            ps = [partial(grp * 8 + BITREV3[j]) for j in range(8)]
            q = [fold(ps[2 * j], ps[2 * j + 1], 4) for j in range(4)]
            part_ref[grp * 8:grp * 8 + 8, :] = fold(fold(q[0], q[1], 2), fold(q[2], q[3], 2), 1)

    def weights(grow, part_ref, wb_ref):
        a = jnp.sum(part_ref[...].T, axis=0, keepdims=True)
        gelu = 0.5 * a * (1.0 + lax.erf(a * math.sqrt(0.5)))
        wb_ref[...] = jnp.broadcast_to(grow * gelu, (npick, npick)).T

    def v_side(u, slot, wb_ref, start_copy):
        accs = [None] * 4
        for k in range(npick):
            if k % 2 == 0:
                start_copy(npick // 2 + k // 2)
            term = wb_ref[k:k + 1, :] * bufs[slot][k, nc:2 * nc, :].astype(F32)
            accs[k % 4] = term if accs[k % 4] is None else accs[k % 4] + term
        out = (accs[0] + accs[1]) + (accs[2] + accs[3])

        for c in range(nc):
            stage_ref[u:u + 1, c * LANES:(c + 1) * LANES] = out[c:c + 1, :]

    parts = (part0, part1)
    wbs = (wb0, wb1)

    @pl.when(i == 0)
    def _():
        first = idx_copy(0, 0)
        first.start()
        first.wait()
        for t in range(ahead):
            for k in range(npick):
                slab_copy(k * tb + t, k, t).start()
        wait(0)
        u_side(h2_ref[0:1, :], 0, parts[0])

    @pl.when(has_next)
    def _():
        idx_copy(i + 1, nxt).start()

    next_base = jnp.where(has_next, nxt, cur) * blk_words - tb

    last = tb // N_SLOTS - 1

    def row_of(ref, next_ref, r, u, reach):
        row = ref[pl.ds(jnp.minimum(r, tb - 1), 1), :]
        over = u + reach - N_SLOTS
        if over >= 0:
            row = jnp.where(r < tb, row, next_ref[over:over + 1, :])
        return row

    def group_of_tokens(j, carry):
        @pl.when(jnp.logical_and(j == last, has_next))
        def _():
            idx_copy(i + 1, nxt).wait()

        for u in range(N_SLOTS):
            t = j * N_SLOTS + u
            ta = t + ahead
            word0 = jnp.where(ta < tb, cur * blk_words, next_base) + ta
            fill = (u + ahead) % N_SLOTS

            def start_copy(k, word0=word0, fill=fill):
                slab_copy(word0 + k * tb, k, fill).start(priority=k % DMA_THREADS)

            wait((u + 1) % N_SLOTS)
            weights(g_ref[pl.ds(t, 1), :], parts[u % 2], wbs[u % 2])
            u_side(row_of(h2_ref, h2n_ref, t + 1, u, 1), (u + 1) % N_SLOTS, parts[(u + 1) % 2],
                   start_copy)
            v_side(u, u, wbs[u % 2], start_copy)
        peer_ref[pl.ds(pl.multiple_of(j * N_SLOTS, N_SLOTS), N_SLOTS), :] = stage_ref[...]
        return carry

    lax.fori_loop(0, last + 1, group_of_tokens, 0)

    @pl.when(jnp.logical_not(has_next))
    def _():
        for s in range(1, ahead):
            wait(s)

    y = xl_ref[...] + gt2_ref[0] * peer_ref[...]
    o_ref[...] = _rms(y, gf_ref[...])


def _peer(idx_t, uv, g, h2, xl, gt2, gf, blocks_per_batch):
    nblk, npick, tb = idx_t.shape
    n, d = h2.shape
    nc = d // LANES
    assert tb % N_SLOTS == 0 and N_SLOTS % 8 == 0 and npick % 8 == 0 and nc % 16 == 0
    assert uv.shape[1:] == (2 * nc, LANES) and uv.dtype == BF16
    row = lambda i: (i, 0)
    next_rows = lambda i: (jnp.minimum(i + 1, nblk - 1) * (tb // 8), 0)
    return pl.pallas_call(
        _peer_kernel,
        grid=(nblk,),
        in_specs=[pl.BlockSpec(memory_space=pl.ANY), pl.BlockSpec(memory_space=pl.ANY),
                  pl.BlockSpec((tb, npick), row), pl.BlockSpec((8, npick), next_rows),
                  pl.BlockSpec((tb, d), row), pl.BlockSpec((8, d), next_rows),
                  pl.BlockSpec((tb, d), row),
                  pl.BlockSpec((1, 1, d), lambda i: (i // blocks_per_batch, 0, 0)),
                  pl.BlockSpec((1, d), lambda i: (0, 0))],
        out_specs=pl.BlockSpec((tb, d), row),
        out_shape=jax.ShapeDtypeStruct((n, d), F32),
        scratch_shapes=[pltpu.SMEM((2 * npick * tb,), jnp.int32), pltpu.SemaphoreType.DMA((2,)),
                        pltpu.SemaphoreType.DMA((N_SLOTS,)), pltpu.VMEM((tb, d), F32),
                        pltpu.VMEM((N_SLOTS, d), F32),
                        pltpu.VMEM((npick, npick), F32), pltpu.VMEM((npick, npick), F32),
                        pltpu.VMEM((npick, LANES), F32), pltpu.VMEM((npick, LANES), F32)]
        + [pltpu.VMEM((npick, 2 * nc, LANES), BF16) for _ in range(N_SLOTS)],
        compiler_params=_params("arbitrary"),
        name="peer",
    )(idx_t.reshape(-1), uv, g, g, h2, h2, xl, gt2, gf)


def _rope_tables(length):
    rows = length // GRID_W
    row = jnp.broadcast_to(jnp.arange(rows)[:, None], (rows, GRID_W)).reshape(-1)
    col = jnp.broadcast_to(jnp.arange(GRID_W)[None, :], (rows, GRID_W)).reshape(-1)
    inv_freq = ROPE_THETA ** (-jnp.arange(ROPE_FREQS, dtype=F32) / ROPE_FREQS)
    ar = row.astype(F32)[:, None] * inv_freq
    ac = col.astype(F32)[:, None] * inv_freq
    cos = jnp.concatenate([jnp.cos(ar), jnp.cos(ar), jnp.cos(ac), jnp.cos(ac)], axis=-1)
    sin = jnp.concatenate([-jnp.sin(ar), jnp.sin(ar), -jnp.sin(ac), jnp.sin(ac)], axis=-1)
    return cos, sin


def kernel(x, c, ctx, c_ctx, w_ada, b_ada, g_norm1, w_in, g_q, g_k, w_fourier, b_fourier,
           w_out, g_norm2, w_query, sub_keys, u_experts, v_experts, g_final):
    b, l, d = x.shape
    assert w_ada.shape[0] == 1, "single-layer configuration only"
    layer = 0
    tb = LANES

    rows = -(-(b + 1) // 8) * 8
    cc = jnp.concatenate([c, c_ctx[None, :], jnp.zeros((rows - b - 1, d), F32)], axis=0)
    mod = _ada(cc, w_ada[layer], b_ada[layer])
    sh1, sc1, gt1, sh2, sc2, gt2 = [m.reshape(b, 1, d) for m in jnp.split(mod[:b], N_MOD, axis=-1)]
    csh1, csc1 = [jnp.broadcast_to(m.reshape(1, 1, d), (b, 1, d))
                  for m in jnp.split(mod[b], N_MOD, axis=-1)[:2]]

    cos, sin = _rope_tables(l)
    g1 = g_norm1[layer].reshape(1, d)
    gq = g_q[layer].reshape(1, HEAD_DIM)
    gk = g_k[layer].reshape(1, HEAD_DIM)
    w_in_bf = w_in[layer].astype(BF16)
    q, k_l, v_l, f = _inproj(x, g1, sh1, sc1, w_in_bf, gq, gk, cos, sin, latent=True)
    lc = ctx.shape[1]
    k_c, v_c = _inproj(ctx, g1, csh1, csc1, w_in_bf, gq, gk, cos[:lc], sin[:lc], latent=False)

    attn = _attention(q, jnp.concatenate([k_l, k_c], axis=1), jnp.concatenate([v_l, v_c], axis=1))

    dft, chan = _dft_tables(l, f.shape[2] // F_GROUPS)
    four = _fourier(f, _fmat(chan, w_fourier[layer]), dft, b_fourier[layer])

    xl, h2, qp = _outproj(attn, four, x, w_out[layer].astype(BF16), gt1,
                          g_norm2[layer].reshape(1, d), sh2, sc2, w_query[layer].astype(BF16))

    g, idx_t = _retrieve(qp, sub_keys[layer].astype(BF16), tb)
    n_exp = u_experts.shape[1]
    uv = jnp.concatenate([u_experts[layer].reshape(n_exp, d // LANES, LANES),
                          v_experts[layer].reshape(n_exp, d // LANES, LANES)], axis=1).astype(BF16)
    out = _peer(idx_t, uv, g, h2.reshape(b * l, d), xl.reshape(b * l, d), gt2,
                g_final.reshape(1, d), l // tb)
    return out.reshape(b, l, d)
```

```python
import functools
import math

import numpy as np
import jax
import jax.numpy as jnp
from jax import lax
from jax.experimental import pallas as pl
from jax.experimental.pallas import tpu as pltpu

GRID_W = 64
HEAD_DIM = 128
N_Q_HEADS = 8
N_KV_HEADS = 2
GQA_GROUP = N_Q_HEADS // N_KV_HEADS
ATTN_WIDTH = N_Q_HEADS * HEAD_DIM
KV_WIDTH = N_KV_HEADS * HEAD_DIM
ROPE_THETA = 10000.0
ROPE_FREQS = HEAD_DIM // 4
F_GROUPS = 4
PEER_HEADS = 8
TOPK = 16
N_MOD = 6
EPS = 1e-6

LANES = 128
VMEM_LIMIT = 56 * 1024 * 1024

F32 = jnp.float32
BF16 = jnp.bfloat16
NT_DIMS = (((1,), (1,)), ((), ()))


def _params(*sem):
    return pltpu.CompilerParams(dimension_semantics=sem, vmem_limit_bytes=VMEM_LIMIT)


def _resident(shape, index_map):
    return pl.BlockSpec(shape, index_map, pipeline_mode=pl.Buffered(1))


def _rms(x, g):
    return x * lax.rsqrt(jnp.mean(x * x, axis=-1, keepdims=True) + EPS) * g


def _ada_kernel(c_ref, w_ref, b_ref, o_ref):
    c = c_ref[...]
    a = (c * jax.nn.sigmoid(c)).astype(BF16)
    o_ref[...] = jnp.dot(a, w_ref[...].astype(BF16), preferred_element_type=F32) + b_ref[...]


def _ada(cc, w, b):
    rows, d = cc.shape
    n = w.shape[1]
    tn = next(t for t in (1024, 512, 256, LANES) if n % t == 0)
    return pl.pallas_call(
        _ada_kernel,
        grid=(n // tn,),
        in_specs=[pl.BlockSpec((rows, d), lambda j: (0, 0)),
                  pl.BlockSpec((d, tn), lambda j: (0, j)),
                  pl.BlockSpec((1, tn), lambda j: (0, j))],
        out_specs=pl.BlockSpec((rows, tn), lambda j: (0, j)),
        out_shape=jax.ShapeDtypeStruct((rows, n), F32),
        compiler_params=_params("parallel"),
        name="ada",
    )(cc, w, b.reshape(1, n))


def _rope(t, cos, sin_signed):
    lane = lax.broadcasted_iota(jnp.int32, t.shape, 1)
    partner = jnp.where((lane % 64) < 32, pltpu.roll(t, 96, 1), pltpu.roll(t, 32, 1))
    return t * cos + partner * sin_signed


def _inproj_kernel(x_ref, g1_ref, sh_ref, sc_ref, w_ref, gq_ref, gk_ref, cos_ref, sin_ref,
                   *out_refs, latent):
    h = (_rms(x_ref[0], g1_ref[...]) * (1.0 + sc_ref[0]) + sh_ref[0]).astype(BF16)
    if latent:
        q_ref, k_ref, v_ref, f_ref = out_refs
        cos, sin = cos_ref[...], sin_ref[...]
        q = jnp.dot(h, w_ref[:, 0:ATTN_WIDTH], preferred_element_type=F32)
        scale = HEAD_DIM ** -0.5
        for j in range(N_Q_HEADS):
            sl = slice(j * HEAD_DIM, (j + 1) * HEAD_DIM)
            q_ref[0, :, sl] = (_rope(_rms(q[:, sl], gq_ref[...]), cos, sin) * scale).astype(BF16)
        k0 = ATTN_WIDTH
    else:
        k_ref, v_ref = out_refs
        k0 = 0
    k = jnp.dot(h, w_ref[:, k0:k0 + KV_WIDTH], preferred_element_type=F32)
    for j in range(N_KV_HEADS):
        sl = slice(j * HEAD_DIM, (j + 1) * HEAD_DIM)
        t = _rms(k[:, sl], gk_ref[...])
        if latent:
            t = _rope(t, cos, sin)
        k_ref[0, :, sl] = t.astype(BF16)
    v_ref[0] = jnp.dot(h, w_ref[:, k0 + KV_WIDTH:k0 + 2 * KV_WIDTH],
                       preferred_element_type=F32).astype(BF16)
    if latent:
        f_ref[0] = jnp.dot(h, w_ref[:, ATTN_WIDTH + 2 * KV_WIDTH:],
                           preferred_element_type=F32).astype(BF16)


def _inproj(x, g1, sh, sc, w_bf, gq, gk, cos, sin, latent):
    b, l, d = x.shape
    tm = min(l, 512)
    wcols = w_bf.shape[1]
    fw = wcols - ATTN_WIDTH - 2 * KV_WIDTH
    row = lambda bi, i: (bi, i, 0)
    per_b = pl.BlockSpec((1, 1, d), lambda bi, i: (bi, 0, 0))
    vec = lambda n: pl.BlockSpec((1, n), lambda bi, i: (0, 0))
    if latent:
        w_spec = _resident((d, wcols), lambda bi, i: (0, 0))
        widths = (ATTN_WIDTH, KV_WIDTH, KV_WIDTH, fw)
    else:
        assert ATTN_WIDTH % (2 * KV_WIDTH) == 0
        w_spec = _resident((d, 2 * KV_WIDTH), lambda bi, i: (0, ATTN_WIDTH // (2 * KV_WIDTH)))
        widths = (KV_WIDTH, KV_WIDTH)
    return pl.pallas_call(
        functools.partial(_inproj_kernel, latent=latent),
        grid=(b, l // tm),
        in_specs=[pl.BlockSpec((1, tm, d), row), vec(d), per_b, per_b, w_spec,
                  vec(HEAD_DIM), vec(HEAD_DIM),
                  pl.BlockSpec((tm, HEAD_DIM), lambda bi, i: (i, 0)),
                  pl.BlockSpec((tm, HEAD_DIM), lambda bi, i: (i, 0))],
        out_specs=[pl.BlockSpec((1, tm, n), row) for n in widths],
        out_shape=[jax.ShapeDtypeStruct((b, l, n), BF16) for n in widths],
        compiler_params=_params("parallel", "parallel"),
        name="inproj_latent" if latent else "inproj_ctx",
    )(x, g1, sh, sc, w_bf, gq, gk, cos, sin)


def _attn_kernel(q_ref, k_ref, v_ref, o_ref):
    k = k_ref[0]
    v = v_ref[0]
    for j in range(GQA_GROUP):
        sl = slice(j * HEAD_DIM, (j + 1) * HEAD_DIM)
        s = lax.dot_general(q_ref[0, :, sl], k, NT_DIMS, preferred_element_type=F32)
        p = jnp.exp(s - jnp.max(s, axis=-1, keepdims=True))
        denom = jnp.sum(p, axis=-1, keepdims=True)
        o = jnp.dot(p.astype(BF16), v, preferred_element_type=F32) / denom
        o_ref[0, :, sl] = o.astype(BF16)


def _attention(q, k_all, v_all):
    b, l, _ = q.shape
    lk = k_all.shape[1]
    tq = min(l, 256)
    gw = GQA_GROUP * HEAD_DIM
    return pl.pallas_call(
        _attn_kernel,
        grid=(b, N_KV_HEADS, l // tq),
        in_specs=[pl.BlockSpec((1, tq, gw), lambda bi, g, i: (bi, i, g)),
                  pl.BlockSpec((1, lk, HEAD_DIM), lambda bi, g, i: (bi, 0, g)),
                  pl.BlockSpec((1, lk, HEAD_DIM), lambda bi, g, i: (bi, 0, g))],
        out_specs=pl.BlockSpec((1, tq, gw), lambda bi, g, i: (bi, i, g)),
        out_shape=jax.ShapeDtypeStruct((b, l, ATTN_WIDTH), BF16),
        compiler_params=_params("parallel", "parallel", "parallel"),
        name="attn",
    )(q, k_all, v_all)


def _dft_tables(l, c):
    kl = (np.arange(l, dtype=np.int64)[:, None] * np.arange(l, dtype=np.int64)[None, :]) % l
    ang = 2.0 * np.pi * kl.astype(np.float64) / l
    pos = np.concatenate([np.cos(ang), -np.sin(ang)], axis=1)
    jc = (np.arange(c, dtype=np.int64)[:, None] * np.arange(c, dtype=np.int64)[None, :]) % c
    angc = 2.0 * np.pi * jc.astype(np.float64) / c
    chan = np.stack([np.cos(angc), np.sin(angc)]) / math.sqrt(l * c)

    def split(t):
        head = t.astype(np.float32)
        return jnp.asarray(head) + jnp.asarray((t - head).astype(np.float32))

    return split(pos).astype(BF16), split(chan)


def _fmat_kernel(cs_ref, w_ref, o_ref):
    c = w_ref.shape[-1]
    w = w_ref[0]
    o_ref[0, :, 0:c] = jnp.dot(cs_ref[0], w, preferred_element_type=F32,
                               precision=lax.Precision.HIGHEST).astype(BF16)
    o_ref[0, :, c:2 * c] = jnp.dot(cs_ref[1], w, preferred_element_type=F32,
                                   precision=lax.Precision.HIGHEST).astype(BF16)


def _fmat(chan, w_f):
    g, c, _ = w_f.shape
    return pl.pallas_call(
        _fmat_kernel,
        grid=(g,),
        in_specs=[pl.BlockSpec((2, c, c), lambda gi: (0, 0, 0)),
                  pl.BlockSpec((1, c, c), lambda gi: (gi, 0, 0))],
        out_specs=pl.BlockSpec((1, c, 2 * c), lambda gi: (gi, 0, 0)),
        out_shape=jax.ShapeDtypeStruct((g, c, 2 * c), BF16),
        compiler_params=_params("parallel"),
        name="fmat",
    )(chan, w_f)


def _fourier_kernel(x_ref, m_ref, dft_ref, b_ref, o_ref, z_ref, *, tr):
    l, c = x_ref.shape[1], x_ref.shape[2]
    z = jnp.dot(x_ref[0], m_ref[0], preferred_element_type=F32)
    z_ref[0:l, :] = z[:, 0:c].astype(BF16)
    z_ref[l:2 * l, :] = z[:, c:2 * c].astype(BF16)
    for r in range(l // tr):
        rows = slice(r * tr, (r + 1) * tr)
        y = jnp.dot(dft_ref[rows, :], z_ref[...], preferred_element_type=F32) + b_ref[0]
        o_ref[0, rows, :] = y.astype(BF16)


def _fourier(f, fm, dft, b_f):
    b, l, fw = f.shape
    c = fw // F_GROUPS
    tr = min(l, 512)
    return pl.pallas_call(
        functools.partial(_fourier_kernel, tr=tr),
        grid=(b, F_GROUPS),
        in_specs=[pl.BlockSpec((1, l, c), lambda bi, g: (bi, 0, g)),
                  pl.BlockSpec((1, c, 2 * c), lambda bi, g: (g, 0, 0)),
                  _resident((l, 2 * l), lambda bi, g: (0, 0)),
                  pl.BlockSpec((1, 1, c), lambda bi, g: (g, 0, 0))],
        out_specs=pl.BlockSpec((1, l, c), lambda bi, g: (bi, 0, g)),
        out_shape=jax.ShapeDtypeStruct((b, l, fw), BF16),
        scratch_shapes=[pltpu.VMEM((2 * l, c), BF16)],
        compiler_params=_params("parallel", "parallel"),
        name="fourier",
    )(f, fm, dft, b_f.reshape(F_GROUPS, 1, c))


def _outproj_kernel(a_ref, f_ref, x_ref, wo_ref, gt1_ref, g2_ref, sh2_ref, sc2_ref, wq_ref,
                    xl_ref, h2_ref, qp_ref):
    aw = a_ref.shape[2]
    y = jnp.dot(a_ref[0], wo_ref[0:aw, :], preferred_element_type=F32)
    y = y + jnp.dot(f_ref[0], wo_ref[aw:, :], preferred_element_type=F32)
    xl = x_ref[0] + gt1_ref[0] * y
    xl_ref[0] = xl
    h2 = _rms(xl, g2_ref[...]) * (1.0 + sc2_ref[0]) + sh2_ref[0]
    h2_ref[0] = h2
    qp = jnp.dot(h2.astype(BF16), wq_ref[...], preferred_element_type=F32)
    for j in range(qp_ref.shape[0]):
        qp_ref[j] = qp[:, j * LANES:(j + 1) * LANES].astype(BF16)


def _outproj(attn, four, x, wo_bf, gt1, g2, sh2, sc2, wq_bf):
    b, l, d = x.shape
    tm = min(l, 256)
    nt = l // tm
    aw, fw = attn.shape[2], four.shape[2]
    qw = wq_bf.shape[1]
    row = lambda bi, i: (bi, i, 0)
    per_b = pl.BlockSpec((1, 1, d), lambda bi, i: (bi, 0, 0))
    return pl.pallas_call(
        _outproj_kernel,
        grid=(b, nt),
        in_specs=[pl.BlockSpec((1, tm, aw), row), pl.BlockSpec((1, tm, fw), row),
                  pl.BlockSpec((1, tm, d), row),
                  _resident((aw + fw, d), lambda bi, i: (0, 0)),
                  per_b, pl.BlockSpec((1, d), lambda bi, i: (0, 0)), per_b, per_b,
                  _resident((d, qw), lambda bi, i: (0, 0))],
        out_specs=[pl.BlockSpec((1, tm, d), row), pl.BlockSpec((1, tm, d), row),
                   pl.BlockSpec((qw // LANES, tm, LANES), lambda bi, i: (0, bi * nt + i, 0))],
        out_shape=[jax.ShapeDtypeStruct((b, l, d), F32), jax.ShapeDtypeStruct((b, l, d), F32),
                   jax.ShapeDtypeStruct((qw // LANES, b * l, LANES), BF16)],
        compiler_params=_params("parallel", "parallel"),
        name="outproj",
    )(attn, four, x, wo_bf, gt1, g2, sh2, sc2, wq_bf)


def _topk_rows(s, k, payload=None):
    rows = s.shape[0]
    iota = lax.broadcasted_iota(jnp.int32, s.shape, 0).astype(F32)
    vals, sel = [], []
    for _ in range(k):
        m = jnp.max(s, axis=0, keepdims=True)
        i = jnp.min(jnp.where(s == m, iota, float(rows)), axis=0, keepdims=True)
        hit = iota == i
        vals.append(m)
        sel.append(i if payload is None else
                   jnp.sum(jnp.where(hit, payload, 0.0), axis=0, keepdims=True))
        s = jnp.where(hit, -jnp.inf, s)
    return jnp.concatenate(vals, axis=0), jnp.concatenate(sel, axis=0)


def _pair_candidates(s1, i1, s2, i2, n_keys):
    assert TOPK == 16
    sub = lax.broadcasted_iota(jnp.int32, (8, s1.shape[1]), 0)
    ids = [i1[0:1] * n_keys + i2, i1[1:2] * n_keys + i2[0:8]]
    vals = [s1[0:1] + s2, s1[1:2] + s2[0:8]]
    for a in range(2, 8):
        keep = sub < TOPK // (a + 1)
        vals.append(jnp.where(keep, s1[a:a + 1] + s2[0:8], -jnp.inf))
        ids.append(i1[a:a + 1] * n_keys + i2[0:8])
    vals.append(s1[8:TOPK] + s2[0:1])
    ids.append(i1[8:TOPK] * n_keys + i2[0:1])
    return jnp.concatenate(vals, axis=0), jnp.concatenate(ids, axis=0)


def _first_stage(keys_ref, q_ref, h, tokens):
    s = jnp.concatenate(
        [lax.dot_general(keys_ref[h, p], q_ref[2 * h + p], NT_DIMS, preferred_element_type=F32)
         for p in range(2)], axis=1)
    return _topk_rows(s, TOPK)


def _second_stage(firsts, tokens, n_keys):
    cands, cidxs = [], []
    for sv, si in firsts:
        cand, cidx = _pair_candidates(sv[:, :tokens], si[:, :tokens],
                                      sv[:, tokens:], si[:, tokens:], float(n_keys))
        cands.append(cand)
        cidxs.append(cidx)
    sf, eidx = _topk_rows(jnp.concatenate(cands, axis=1), TOPK,
                          payload=jnp.concatenate(cidxs, axis=1))
    e = jnp.exp(sf - sf[0:1])
    return e / jnp.sum(e, axis=0, keepdims=True), eidx


N_SLOTS = 8
DMA_THREADS = 2
ROUTE_PHASES = 3


def _peer_kernel(uv_hbm, sel_ref, keys_ref, q0_ref, qn_ref, h2_ref, h2n_ref, xl_ref, gt2_ref, gf_ref,
                 o_ref, idx_smem, idx_sem, sem, peer_ref, stage_ref, p2_ref, g_s, gs_ref, is_ref,
                 ist_ref, sva, sia, svb, sib, wb0, wb1, part0, part1, *bufs):
    tb, d = h2_ref.shape
    npick = gs_ref.shape[0]
    n_keys = keys_ref.shape[2]
    n_pairs = keys_ref.shape[0] // 2
    nc = d // LANES
    i = pl.program_id(0)
    nblk = pl.num_programs(0)
    cur = i % 2
    nxt = 1 - cur
    has_next = i + 1 < nblk
    ahead = N_SLOTS - 1

    def route_first(q_ref, hp, which):
        sv, si = _first_stage(keys_ref, q_ref, 2 * hp + which, tb)
        dst_v, dst_i = (sva, sia) if which == 0 else (svb, sib)
        dst_v[...] = sv
        dst_i[...] = si

    def route_second(hp):
        g, eidx = _second_stage([(sva[...], sia[...]), (svb[...], sib[...])], tb, n_keys)
        for hh in range(2):
            rows = pl.ds(pl.multiple_of((2 * hp + hh) * TOPK, TOPK), TOPK)
            gs_ref[rows, :] = g[:, hh * tb:(hh + 1) * tb]
            is_ref[rows, :] = eidx[:, hh * tb:(hh + 1) * tb]

    def idx_copy(half):
        return pltpu.make_async_copy(
            ist_ref, idx_smem.at[pl.ds(pl.multiple_of(half * tb, tb), tb)], idx_sem.at[half])

    def route_finish(half):
        g_s[half] = gs_ref[...].T
        ist_ref[...] = is_ref[...].T.astype(jnp.int32)
        idx_copy(half).start()

    def slab_copy(row, k, slot):
        return pltpu.make_async_copy(uv_hbm.at[idx_smem[row, k]], bufs[slot].at[k], sem.at[slot])

    def wait(slot):
        pltpu.make_async_copy(uv_hbm.at[pl.ds(0, npick)], bufs[slot], sem.at[slot]).wait()

    def u_side(xrow, slot, part_ref, start_copy=None):
        x = jnp.concatenate([xrow[:, c * LANES:(c + 1) * LANES] for c in range(nc)], axis=0)

        def partial(k):
            p = bufs[slot][k, 0:nc, :].astype(F32) * x
            acc = p[0:8]
            for r in range(1, nc // 8):
                acc = acc + p[8 * r:8 * r + 8]
            return acc

        for grp in range(npick // 8):
            if start_copy is not None:
                for j in range(4):
                    start_copy(grp * 4 + j)
            ps = jnp.concatenate([partial(grp * 8 + j) for j in range(8)], axis=0)
            p2_ref[grp * 64:grp * 64 + 64, :] = ps.astype(BF16)
        part_ref[...] = jnp.dot(sel_ref[...], p2_ref[...], preferred_element_type=F32)

    def weights(grow, part_ref, wb_ref):
        a = jnp.sum(part_ref[...].T, axis=0, keepdims=True)
        gelu = 0.5 * a * (1.0 + lax.erf(a * math.sqrt(0.5)))
        wb_ref[...] = jnp.broadcast_to(grow * gelu, (npick, npick)).T

    def v_side(u, slot, wb_ref, start_copy):
        accs = [None] * 4
        for k in range(npick):
            if k % 2 == 0:
                start_copy(npick // 2 + k // 2)
            term = wb_ref[k:k + 1, :] * bufs[slot][k, nc:2 * nc, :].astype(F32)
            accs[k % 4] = term if accs[k % 4] is None else accs[k % 4] + term
        out = (accs[0] + accs[1]) + (accs[2] + accs[3])
        for c in range(nc):
            stage_ref[u:u + 1, c * LANES:(c + 1) * LANES] = out[c:c + 1, :]

    parts = (part0, part1)
    wbs = (wb0, wb1)

    @pl.when(i == 0)
    def _():
        def pair(hp, carry):
            route_first(q0_ref, hp, 0)
            route_first(q0_ref, hp, 1)
            route_second(hp)
            return carry

        lax.fori_loop(0, n_pairs, pair, 0)
        route_finish(0)
        idx_copy(0).wait()
        for t in range(ahead):
            for k in range(npick):
                slab_copy(t, k, t).start(priority=k % DMA_THREADS)
        wait(0)
        u_side(h2_ref[0:1, :], 0, parts[0])

    next_base = jnp.where(has_next, nxt, cur) * tb - tb

    last = tb // N_SLOTS - 1
    n_route = ROUTE_PHASES * n_pairs
    assert n_route + 1 <= last

    def row_of(ref, next_ref, r, u, reach):
        row = ref[pl.ds(jnp.minimum(r, tb - 1), 1), :]
        over = u + reach - N_SLOTS
        if over >= 0:
            row = jnp.where(r < tb, row, next_ref[over:over + 1, :])
        return row

    def group_of_tokens(j, carry):
        routing = jnp.logical_and(has_next, j < n_route)
        for phase in range(ROUTE_PHASES):
            @pl.when(jnp.logical_and(routing, j % ROUTE_PHASES == phase))
            def _(phase=phase):
                if phase < 2:
                    route_first(qn_ref, j // ROUTE_PHASES, phase)
                else:
                    route_second(j // ROUTE_PHASES)

        @pl.when(jnp.logical_and(has_next, j == n_route))
        def _():
            route_finish(nxt)

        @pl.when(jnp.logical_and(has_next, j == last))
        def _():
            idx_copy(nxt).wait()

        for u in range(N_SLOTS):
            t = j * N_SLOTS + u
            ta = t + ahead
            row = jnp.where(ta < tb, cur * tb, next_base) + ta
            fill = (u + ahead) % N_SLOTS

            def start_copy(k, row=row, fill=fill):
                slab_copy(row, k, fill).start(priority=k % DMA_THREADS)

            wait((u + 1) % N_SLOTS)
            weights(g_s[cur, pl.ds(t, 1), :], parts[u % 2], wbs[u % 2])
            u_side(row_of(h2_ref, h2n_ref, t + 1, u, 1), (u + 1) % N_SLOTS, parts[(u + 1) % 2],
                   start_copy)
            v_side(u, u, wbs[u % 2], start_copy)
        peer_ref[pl.ds(pl.multiple_of(j * N_SLOTS, N_SLOTS), N_SLOTS), :] = stage_ref[...]
        return carry

    lax.fori_loop(0, last + 1, group_of_tokens, 0)

    @pl.when(jnp.logical_not(has_next))
    def _():
        for s in range(1, ahead):
            wait(s)

    y = xl_ref[...] + gt2_ref[0] * peer_ref[...]
    o_ref[...] = _rms(y, gf_ref[...])


def _peer(qp, keys_bf, uv, h2, xl, gt2, gf, blocks_per_batch, tb):
    nq, n, _ = qp.shape
    _, d = h2.shape
    nc = d // LANES
    nblk = n // tb
    npick = (nq // 2) * TOPK
    assert tb == LANES and npick == LANES and tb % N_SLOTS == 0 and N_SLOTS % 8 == 0
    assert nc % 16 == 0 and keys_bf.shape[0] % 2 == 0
    assert uv.shape[1:] == (2 * nc, LANES) and uv.dtype == BF16
    sel = jnp.asarray(np.kron(np.eye(npick), np.ones((1, 8))), dtype=BF16)
    row = lambda i: (i, 0)
    following = lambda i: jnp.minimum(i + 1, nblk - 1)
    return pl.pallas_call(
        _peer_kernel,
        grid=(nblk,),
        in_specs=[pl.BlockSpec(memory_space=pl.ANY),
                  pl.BlockSpec((npick, 8 * npick), lambda i: (0, 0)),
                  pl.BlockSpec(keys_bf.shape, lambda i: (0, 0, 0, 0)),
                  pl.BlockSpec((nq, tb, LANES), lambda i: (0, 0, 0)),
                  pl.BlockSpec((nq, tb, LANES), lambda i: (0, following(i), 0)),
                  pl.BlockSpec((tb, d), row),
                  pl.BlockSpec((8, d), lambda i: (following(i) * (tb // 8), 0)),
                  pl.BlockSpec((tb, d), row),
                  pl.BlockSpec((1, 1, d), lambda i: (i // blocks_per_batch, 0, 0)),
                  pl.BlockSpec((1, d), lambda i: (0, 0))],
        out_specs=pl.BlockSpec((tb, d), row),
        out_shape=jax.ShapeDtypeStruct((n, d), F32),
        scratch_shapes=[pltpu.SMEM((2 * tb, npick), jnp.int32), pltpu.SemaphoreType.DMA((2,)),
                        pltpu.SemaphoreType.DMA((N_SLOTS,)), pltpu.VMEM((tb, d), F32),
                        pltpu.VMEM((N_SLOTS, d), F32), pltpu.VMEM((8 * npick, LANES), BF16),
                        pltpu.VMEM((2, tb, npick), F32), pltpu.VMEM((npick, tb), F32),
                        pltpu.VMEM((npick, tb), F32), pltpu.VMEM((tb, npick), jnp.int32)]
        + [pltpu.VMEM((TOPK, 2 * tb), F32) for _ in range(4)]
        + [pltpu.VMEM((npick, npick), F32), pltpu.VMEM((npick, npick), F32),
           pltpu.VMEM((npick, LANES), F32), pltpu.VMEM((npick, LANES), F32)]
        + [pltpu.VMEM((npick, 2 * nc, LANES), BF16) for _ in range(N_SLOTS)],
        compiler_params=_params("arbitrary"),
        name="peer",
    )(uv, sel, keys_bf, qp, qp, h2, h2, xl, gt2, gf)


def _rope_tables(length):
    rows = length // GRID_W
    row = jnp.broadcast_to(jnp.arange(rows)[:, None], (rows, GRID_W)).reshape(-1)
    col = jnp.broadcast_to(jnp.arange(GRID_W)[None, :], (rows, GRID_W)).reshape(-1)
    inv_freq = ROPE_THETA ** (-jnp.arange(ROPE_FREQS, dtype=F32) / ROPE_FREQS)
    ar = row.astype(F32)[:, None] * inv_freq
    ac = col.astype(F32)[:, None] * inv_freq
    cos = jnp.concatenate([jnp.cos(ar), jnp.cos(ar), jnp.cos(ac), jnp.cos(ac)], axis=-1)
    sin = jnp.concatenate([-jnp.sin(ar), jnp.sin(ar), -jnp.sin(ac), jnp.sin(ac)], axis=-1)
    return cos, sin


def kernel(x, c, ctx, c_ctx, w_ada, b_ada, g_norm1, w_in, g_q, g_k, w_fourier, b_fourier,
           w_out, g_norm2, w_query, sub_keys, u_experts, v_experts, g_final):
    b, l, d = x.shape
    assert w_ada.shape[0] == 1, "single-layer configuration only"
    layer = 0
    tb = LANES

    rows = -(-(b + 1) // 8) * 8
    cc = jnp.concatenate([c, c_ctx[None, :], jnp.zeros((rows - b - 1, d), F32)], axis=0)
    mod = _ada(cc, w_ada[layer], b_ada[layer])
    sh1, sc1, gt1, sh2, sc2, gt2 = [m.reshape(b, 1, d) for m in jnp.split(mod[:b], N_MOD, axis=-1)]
    csh1, csc1 = [jnp.broadcast_to(m.reshape(1, 1, d), (b, 1, d))
                  for m in jnp.split(mod[b], N_MOD, axis=-1)[:2]]

    cos, sin = _rope_tables(l)
    g1 = g_norm1[layer].reshape(1, d)
    gq = g_q[layer].reshape(1, HEAD_DIM)
    gk = g_k[layer].reshape(1, HEAD_DIM)
    w_in_bf = w_in[layer].astype(BF16)
    q, k_l, v_l, f = _inproj(x, g1, sh1, sc1, w_in_bf, gq, gk, cos, sin, latent=True)
    lc = ctx.shape[1]
    k_c, v_c = _inproj(ctx, g1, csh1, csc1, w_in_bf, gq, gk, cos[:lc], sin[:lc], latent=False)

    attn = _attention(q, jnp.concatenate([k_l, k_c], axis=1), jnp.concatenate([v_l, v_c], axis=1))

    dft, chan = _dft_tables(l, f.shape[2] // F_GROUPS)
    four = _fourier(f, _fmat(chan, w_fourier[layer]), dft, b_fourier[layer])

    xl, h2, qp = _outproj(attn, four, x, w_out[layer].astype(BF16), gt1,
                          g_norm2[layer].reshape(1, d), sh2, sc2, w_query[layer].astype(BF16))

    n_exp = u_experts.shape[1]
    uv = jnp.concatenate([u_experts[layer].reshape(n_exp, d // LANES, LANES),
                          v_experts[layer].reshape(n_exp, d // LANES, LANES)], axis=1).astype(BF16)
    out = _peer(qp, sub_keys[layer].astype(BF16), uv, h2.reshape(b * l, d), xl.reshape(b * l, d),
                gt2, g_final.reshape(1, d), l // tb, tb)
    return out.reshape(b, l, d)
```

```python
import functools
import math

import numpy as np
import jax
import jax.numpy as jnp
from jax import lax
from jax.experimental import pallas as pl
from jax.experimental.pallas import tpu as pltpu

GRID_W = 64
HEAD_DIM = 128
N_Q_HEADS = 8
N_KV_HEADS = 2
GQA_GROUP = N_Q_HEADS // N_KV_HEADS
ATTN_WIDTH = N_Q_HEADS * HEAD_DIM
KV_WIDTH = N_KV_HEADS * HEAD_DIM
ROPE_THETA = 10000.0
ROPE_FREQS = HEAD_DIM // 4
F_GROUPS = 4
PEER_HEADS = 8
TOPK = 16
N_MOD = 6
EPS = 1e-6

LANES = 128
VMEM_LIMIT = 56 * 1024 * 1024

F32 = jnp.float32
BF16 = jnp.bfloat16
NT_DIMS = (((1,), (1,)), ((), ()))


def _params(*sem):
    return pltpu.CompilerParams(dimension_semantics=sem, vmem_limit_bytes=VMEM_LIMIT)


def _resident(shape, index_map):
    return pl.BlockSpec(shape, index_map, pipeline_mode=pl.Buffered(1))


def _rms(x, g):
    return x * lax.rsqrt(jnp.mean(x * x, axis=-1, keepdims=True) + EPS) * g


def _ada_kernel(c_ref, w_ref, b_ref, o_ref):
    c = c_ref[...]
    a = (c * jax.nn.sigmoid(c)).astype(BF16)
    o_ref[...] = jnp.dot(a, w_ref[...].astype(BF16), preferred_element_type=F32) + b_ref[...]


def _ada(cc, w, b):
    rows, d = cc.shape
    n = w.shape[1]
    tn = next(t for t in (1024, 512, 256, LANES) if n % t == 0)
    return pl.pallas_call(
        _ada_kernel,
        grid=(n // tn,),
        in_specs=[pl.BlockSpec((rows, d), lambda j: (0, 0)),
                  pl.BlockSpec((d, tn), lambda j: (0, j)),
                  pl.BlockSpec((1, tn), lambda j: (0, j))],
        out_specs=pl.BlockSpec((rows, tn), lambda j: (0, j)),
        out_shape=jax.ShapeDtypeStruct((rows, n), F32),
        compiler_params=_params("parallel"),
        name="ada",
    )(cc, w, b.reshape(1, n))


def _rope(t, cos, sin_signed):
    lane = lax.broadcasted_iota(jnp.int32, t.shape, 1)
    partner = jnp.where((lane % 64) < 32, pltpu.roll(t, 96, 1), pltpu.roll(t, 32, 1))
    return t * cos + partner * sin_signed


def _inproj_kernel(x_ref, g1_ref, sh_ref, sc_ref, w_ref, gq_ref, gk_ref, cos_ref, sin_ref,
                   *out_refs, latent):
    h = (_rms(x_ref[0], g1_ref[...]) * (1.0 + sc_ref[0]) + sh_ref[0]).astype(BF16)
    if latent:
        q_ref, k_ref, v_ref, f_ref = out_refs
        cos, sin = cos_ref[...], sin_ref[...]
        q = jnp.dot(h, w_ref[:, 0:ATTN_WIDTH], preferred_element_type=F32)
        scale = HEAD_DIM ** -0.5
        for j in range(N_Q_HEADS):
            sl = slice(j * HEAD_DIM, (j + 1) * HEAD_DIM)
            q_ref[0, :, sl] = (_rope(_rms(q[:, sl], gq_ref[...]), cos, sin) * scale).astype(BF16)
        k0 = ATTN_WIDTH
    else:
        k_ref, v_ref = out_refs
        k0 = 0
    k = jnp.dot(h, w_ref[:, k0:k0 + KV_WIDTH], preferred_element_type=F32)
    for j in range(N_KV_HEADS):
        sl = slice(j * HEAD_DIM, (j + 1) * HEAD_DIM)
        t = _rms(k[:, sl], gk_ref[...])
        if latent:
            t = _rope(t, cos, sin)
        k_ref[0, :, sl] = t.astype(BF16)
    v_ref[0] = jnp.dot(h, w_ref[:, k0 + KV_WIDTH:k0 + 2 * KV_WIDTH],
                       preferred_element_type=F32).astype(BF16)
    if latent:
        f_ref[0] = jnp.dot(h, w_ref[:, ATTN_WIDTH + 2 * KV_WIDTH:],
                           preferred_element_type=F32).astype(BF16)


def _inproj(x, g1, sh, sc, w_bf, gq, gk, cos, sin, latent):
    b, l, d = x.shape
    tm = min(l, 512)
    wcols = w_bf.shape[1]
    fw = wcols - ATTN_WIDTH - 2 * KV_WIDTH
    row = lambda bi, i: (bi, i, 0)
    per_b = pl.BlockSpec((1, 1, d), lambda bi, i: (bi, 0, 0))
    vec = lambda n: pl.BlockSpec((1, n), lambda bi, i: (0, 0))
    if latent:
        w_spec = _resident((d, wcols), lambda bi, i: (0, 0))
        widths = (ATTN_WIDTH, KV_WIDTH, KV_WIDTH, fw)
    else:
        assert ATTN_WIDTH % (2 * KV_WIDTH) == 0
        w_spec = _resident((d, 2 * KV_WIDTH), lambda bi, i: (0, ATTN_WIDTH // (2 * KV_WIDTH)))
        widths = (KV_WIDTH, KV_WIDTH)
    return pl.pallas_call(
        functools.partial(_inproj_kernel, latent=latent),
        grid=(b, l // tm),
        in_specs=[pl.BlockSpec((1, tm, d), row), vec(d), per_b, per_b, w_spec,
                  vec(HEAD_DIM), vec(HEAD_DIM),
                  pl.BlockSpec((tm, HEAD_DIM), lambda bi, i: (i, 0)),
                  pl.BlockSpec((tm, HEAD_DIM), lambda bi, i: (i, 0))],
        out_specs=[pl.BlockSpec((1, tm, n), row) for n in widths],
        out_shape=[jax.ShapeDtypeStruct((b, l, n), BF16) for n in widths],
        compiler_params=_params("parallel", "parallel"),
        name="inproj_latent" if latent else "inproj_ctx",
    )(x, g1, sh, sc, w_bf, gq, gk, cos, sin)


def _attn_kernel(q_ref, k_ref, v_ref, o_ref):
    k = k_ref[0]
    v = v_ref[0]
    for j in range(GQA_GROUP):
        sl = slice(j * HEAD_DIM, (j + 1) * HEAD_DIM)
        s = lax.dot_general(q_ref[0, :, sl], k, NT_DIMS, preferred_element_type=F32)
        p = jnp.exp(s - jnp.max(s, axis=-1, keepdims=True))
        denom = jnp.sum(p, axis=-1, keepdims=True)
        o = jnp.dot(p.astype(BF16), v, preferred_element_type=F32) / denom
        o_ref[0, :, sl] = o.astype(BF16)


def _attention(q, k_all, v_all):
    b, l, _ = q.shape
    lk = k_all.shape[1]
    tq = min(l, 256)
    gw = GQA_GROUP * HEAD_DIM
    return pl.pallas_call(
        _attn_kernel,
        grid=(b, N_KV_HEADS, l // tq),
        in_specs=[pl.BlockSpec((1, tq, gw), lambda bi, g, i: (bi, i, g)),
                  pl.BlockSpec((1, lk, HEAD_DIM), lambda bi, g, i: (bi, 0, g)),
                  pl.BlockSpec((1, lk, HEAD_DIM), lambda bi, g, i: (bi, 0, g))],
        out_specs=pl.BlockSpec((1, tq, gw), lambda bi, g, i: (bi, i, g)),
        out_shape=jax.ShapeDtypeStruct((b, l, ATTN_WIDTH), BF16),
        compiler_params=_params("parallel", "parallel", "parallel"),
        name="attn",
    )(q, k_all, v_all)


def _dft_tables(l, c):
    kl = (np.arange(l, dtype=np.int64)[:, None] * np.arange(l, dtype=np.int64)[None, :]) % l
    ang = 2.0 * np.pi * kl.astype(np.float64) / l
    pos = np.concatenate([np.cos(ang), -np.sin(ang)], axis=1)
    jc = (np.arange(c, dtype=np.int64)[:, None] * np.arange(c, dtype=np.int64)[None, :]) % c
    angc = 2.0 * np.pi * jc.astype(np.float64) / c
    chan = np.stack([np.cos(angc), np.sin(angc)]) / math.sqrt(l * c)

    def split(t):
        head = t.astype(np.float32)
        return jnp.asarray(head) + jnp.asarray((t - head).astype(np.float32))

    return split(pos).astype(BF16), split(chan)


def _fmat_kernel(cs_ref, w_ref, o_ref):
    c = w_ref.shape[-1]
    w = w_ref[0]
    o_ref[0, :, 0:c] = jnp.dot(cs_ref[0], w, preferred_element_type=F32,
                               precision=lax.Precision.HIGHEST).astype(BF16)
    o_ref[0, :, c:2 * c] = jnp.dot(cs_ref[1], w, preferred_element_type=F32,
                                   precision=lax.Precision.HIGHEST).astype(BF16)


def _fmat(chan, w_f):
    g, c, _ = w_f.shape
    return pl.pallas_call(
        _fmat_kernel,
        grid=(g,),
        in_specs=[pl.BlockSpec((2, c, c), lambda gi: (0, 0, 0)),
                  pl.BlockSpec((1, c, c), lambda gi: (gi, 0, 0))],
        out_specs=pl.BlockSpec((1, c, 2 * c), lambda gi: (gi, 0, 0)),
        out_shape=jax.ShapeDtypeStruct((g, c, 2 * c), BF16),
        compiler_params=_params("parallel"),
        name="fmat",
    )(chan, w_f)


def _fourier_kernel(x_ref, m_ref, dft_ref, b_ref, o_ref, z_ref, *, tr):
    l, c = x_ref.shape[1], x_ref.shape[2]
    z = jnp.dot(x_ref[0], m_ref[0], preferred_element_type=F32)
    z_ref[0:l, :] = z[:, 0:c].astype(BF16)
    z_ref[l:2 * l, :] = z[:, c:2 * c].astype(BF16)
    for r in range(l // tr):
        rows = slice(r * tr, (r + 1) * tr)
        y = jnp.dot(dft_ref[rows, :], z_ref[...], preferred_element_type=F32) + b_ref[0]
        o_ref[0, rows, :] = y.astype(BF16)


def _fourier(f, fm, dft, b_f):
    b, l, fw = f.shape
    c = fw // F_GROUPS
    tr = min(l, 512)
    return pl.pallas_call(
        functools.partial(_fourier_kernel, tr=tr),
        grid=(b, F_GROUPS),
        in_specs=[pl.BlockSpec((1, l, c), lambda bi, g: (bi, 0, g)),
                  pl.BlockSpec((1, c, 2 * c), lambda bi, g: (g, 0, 0)),
                  _resident((l, 2 * l), lambda bi, g: (0, 0)),
                  pl.BlockSpec((1, 1, c), lambda bi, g: (g, 0, 0))],
        out_specs=pl.BlockSpec((1, l, c), lambda bi, g: (bi, 0, g)),
        out_shape=jax.ShapeDtypeStruct((b, l, fw), BF16),
        scratch_shapes=[pltpu.VMEM((2 * l, c), BF16)],
        compiler_params=_params("parallel", "parallel"),
        name="fourier",
    )(f, fm, dft, b_f.reshape(F_GROUPS, 1, c))


def _outproj_kernel(a_ref, f_ref, x_ref, wo_ref, gt1_ref, g2_ref, sh2_ref, sc2_ref, wq_ref,
                    xl_ref, h2_ref, qp_ref):
    aw = a_ref.shape[2]
    y = jnp.dot(a_ref[0], wo_ref[0:aw, :], preferred_element_type=F32)
    y = y + jnp.dot(f_ref[0], wo_ref[aw:, :], preferred_element_type=F32)
    xl = x_ref[0] + gt1_ref[0] * y
    xl_ref[0] = xl
    h2 = _rms(xl, g2_ref[...]) * (1.0 + sc2_ref[0]) + sh2_ref[0]
    h2_ref[0] = h2
    qp = jnp.dot(h2.astype(BF16), wq_ref[...], preferred_element_type=F32)
    for j in range(qp_ref.shape[0]):
        qp_ref[j] = qp[:, j * LANES:(j + 1) * LANES].astype(BF16)


def _outproj(attn, four, x, wo_bf, gt1, g2, sh2, sc2, wq_bf):
    b, l, d = x.shape
    tm = min(l, 256)
    nt = l // tm
    aw, fw = attn.shape[2], four.shape[2]
    qw = wq_bf.shape[1]
    row = lambda bi, i: (bi, i, 0)
    per_b = pl.BlockSpec((1, 1, d), lambda bi, i: (bi, 0, 0))
    return pl.pallas_call(
        _outproj_kernel,
        grid=(b, nt),
        in_specs=[pl.BlockSpec((1, tm, aw), row), pl.BlockSpec((1, tm, fw), row),
                  pl.BlockSpec((1, tm, d), row),
                  _resident((aw + fw, d), lambda bi, i: (0, 0)),
                  per_b, pl.BlockSpec((1, d), lambda bi, i: (0, 0)), per_b, per_b,
                  _resident((d, qw), lambda bi, i: (0, 0))],
        out_specs=[pl.BlockSpec((1, tm, d), row), pl.BlockSpec((1, tm, d), row),
                   pl.BlockSpec((qw // LANES, tm, LANES), lambda bi, i: (0, bi * nt + i, 0))],
        out_shape=[jax.ShapeDtypeStruct((b, l, d), F32), jax.ShapeDtypeStruct((b, l, d), F32),
                   jax.ShapeDtypeStruct((qw // LANES, b * l, LANES), BF16)],
        compiler_params=_params("parallel", "parallel"),
        name="outproj",
    )(attn, four, x, wo_bf, gt1, g2, sh2, sc2, wq_bf)


def _topk_rows(s, k, payload=None):
    rows = s.shape[0]
    iota = lax.broadcasted_iota(jnp.int32, s.shape, 0).astype(F32)
    vals, sel = [], []
    for _ in range(k):
        m = jnp.max(s, axis=0, keepdims=True)
        i = jnp.min(jnp.where(s == m, iota, float(rows)), axis=0, keepdims=True)
        hit = iota == i
        vals.append(m)
        sel.append(i if payload is None else
                   jnp.sum(jnp.where(hit, payload, 0.0), axis=0, keepdims=True))
        s = jnp.where(hit, -jnp.inf, s)
    return jnp.concatenate(vals, axis=0), jnp.concatenate(sel, axis=0)


def _pair_candidates(s1, i1, s2, i2, n_keys):
    assert TOPK == 16
    sub = lax.broadcasted_iota(jnp.int32, (8, s1.shape[1]), 0)
    ids = [i1[0:1] * n_keys + i2, i1[1:2] * n_keys + i2[0:8]]
    vals = [s1[0:1] + s2, s1[1:2] + s2[0:8]]
    for a in range(2, 8):
        keep = sub < TOPK // (a + 1)
        vals.append(jnp.where(keep, s1[a:a + 1] + s2[0:8], -jnp.inf))
        ids.append(i1[a:a + 1] * n_keys + i2[0:8])
    vals.append(s1[8:TOPK] + s2[0:1])
    ids.append(i1[8:TOPK] * n_keys + i2[0:1])
    return jnp.concatenate(vals, axis=0), jnp.concatenate(ids, axis=0)


def _retrieve_kernel(q_ref, keys_ref, g_ref, idx_ref, gs_ref, is_ref):
    n_keys = keys_ref.shape[2]

    tokens = q_ref.shape[1]

    def head_pair(hp, carry):
        cands, cidxs = [], []
        for hh in range(2):
            h = 2 * hp + hh
            s = jnp.concatenate(
                [lax.dot_general(keys_ref[h, p], q_ref[2 * h + p], NT_DIMS,
                                 preferred_element_type=F32) for p in range(2)],
                axis=1)
            sv, si = _topk_rows(s, TOPK)
            cand, cidx = _pair_candidates(sv[:, :tokens], si[:, :tokens],
                                          sv[:, tokens:], si[:, tokens:], float(n_keys))
            cands.append(cand)
            cidxs.append(cidx)
        sf, eidx = _topk_rows(jnp.concatenate(cands, axis=1), TOPK,
                              payload=jnp.concatenate(cidxs, axis=1))
        e = jnp.exp(sf - sf[0:1])
        g = e / jnp.sum(e, axis=0, keepdims=True)
        for hh in range(2):
            rows = pl.ds(pl.multiple_of((2 * hp + hh) * TOPK, TOPK), TOPK)
            lanes = slice(hh * tokens, (hh + 1) * tokens)
            gs_ref[rows, :] = g[:, lanes]
            is_ref[rows, :] = eidx[:, lanes].astype(jnp.int32)
        return carry

    assert PEER_HEADS % 2 == 0
    lax.fori_loop(0, PEER_HEADS // 2, head_pair, 0)
    g_ref[...] = gs_ref[...].T
    idx_ref[0] = is_ref[...]


def _retrieve(qp, keys_bf, tb):
    nq, n, _ = qp.shape
    npick = PEER_HEADS * TOPK
    assert tb == LANES and npick == LANES
    nblk = n // tb
    return pl.pallas_call(
        _retrieve_kernel,
        grid=(nblk,),
        in_specs=[pl.BlockSpec((nq, tb, LANES), lambda i: (0, i, 0)),
                  pl.BlockSpec(keys_bf.shape, lambda i: (0, 0, 0, 0))],
        out_specs=[pl.BlockSpec((tb, npick), lambda i: (i, 0)),
                   pl.BlockSpec((1, npick, tb), lambda i: (i, 0, 0))],
        out_shape=[jax.ShapeDtypeStruct((n, npick), F32),
                   jax.ShapeDtypeStruct((nblk, npick, tb), jnp.int32)],
        scratch_shapes=[pltpu.VMEM((npick, tb), F32), pltpu.VMEM((npick, tb), jnp.int32)],
        compiler_params=_params("parallel"),
        name="retrieve",
    )(qp, keys_bf)


N_SLOTS = 8
BITREV3 = (0, 4, 2, 6, 1, 5, 3, 7)
DMA_THREADS = 2


def _peer_kernel(idx_hbm, uv_hbm, g_ref, h2_ref, h2n_ref, xl_ref, gt2_ref, gf_ref, o_ref,
                 idx_smem, idx_sem, sem, peer_ref, stage_ref, wb0, wb1, part0, part1, *bufs):
    tb, d = h2_ref.shape
    npick = g_ref.shape[1]
    nc = d // LANES
    blk_words = npick * tb
    i = pl.program_id(0)
    nblk = pl.num_programs(0)
    cur = i % 2
    nxt = 1 - cur
    has_next = i + 1 < nblk
    ahead = N_SLOTS - 1

    def idx_copy(blk, half):
        return pltpu.make_async_copy(
            idx_hbm.at[pl.ds(pl.multiple_of(blk * blk_words, blk_words), blk_words)],
            idx_smem.at[pl.ds(pl.multiple_of(half * blk_words, blk_words), blk_words)],
            idx_sem.at[half])

    def slab_copy(word, k, slot):
        return pltpu.make_async_copy(uv_hbm.at[idx_smem[word]], bufs[slot].at[k], sem.at[slot])

    def wait(slot):
        pltpu.make_async_copy(uv_hbm.at[pl.ds(0, npick)], bufs[slot], sem.at[slot]).wait()

    sub = lax.broadcasted_iota(jnp.int32, (8, LANES), 0)

    def fold(a, b, dist):
        m = (sub & dist) == 0
        if 2 * dist == 8:
            return jnp.where(m, a, b) + pltpu.roll(jnp.where(m, b, a), dist, 0)
        return (jnp.where(m, a, b)
                + jnp.where(m, pltpu.roll(a, 8 - dist, 0), pltpu.roll(b, dist, 0)))

    def u_side(xrow, slot, part_ref, start_copy=None):
        x = jnp.concatenate([xrow[:, c * LANES:(c + 1) * LANES] for c in range(nc)], axis=0)

        def partial(k):
            p = bufs[slot][k, 0:nc, :].astype(F32) * x
            acc = p[0:8]
            for r in range(1, nc // 8):
                acc = acc + p[8 * r:8 * r + 8]
            return acc

        for grp in range(npick // 8):
            if start_copy is not None:
                for j in range(4):
                    start_copy(grp * 4 + j)
            ps = [partial(grp * 8 + BITREV3[j]) for j in range(8)]
            q = [fold(ps[2 * j], ps[2 * j + 1], 4) for j in range(4)]
            part_ref[grp * 8:grp * 8 + 8, :] = fold(fold(q[0], q[1], 2), fold(q[2], q[3], 2), 1)

    def weights(grow, part_ref, wb_ref):
        a = jnp.sum(part_ref[...].T, axis=0, keepdims=True)
        gelu = 0.5 * a * (1.0 + lax.erf(a * math.sqrt(0.5)))
        wb_ref[...] = jnp.broadcast_to(grow * gelu, (npick, npick)).T

    def v_side(u, slot, wb_ref, start_copy):
        accs = [None] * 4
        for k in range(npick):
            if k % 2 == 0:
                start_copy(npick // 2 + k // 2)
            term = wb_ref[k:k + 1, :] * bufs[slot][k, nc:2 * nc, :].astype(F32)
            accs[k % 4] = term if accs[k % 4] is None else accs[k % 4] + term
        out = (accs[0] + accs[1]) + (accs[2] + accs[3])

        for c in range(nc):
            stage_ref[u:u + 1, c * LANES:(c + 1) * LANES] = out[c:c + 1, :]

    parts = (part0, part1)
    wbs = (wb0, wb1)

    @pl.when(i == 0)
    def _():
        first = idx_copy(0, 0)
        first.start()
        first.wait()
        for t in range(ahead):
            for k in range(npick):
                slab_copy(k * tb + t, k, t).start()
        wait(0)
        u_side(h2_ref[0:1, :], 0, parts[0])

    @pl.when(has_next)
    def _():
        idx_copy(i + 1, nxt).start()

    next_base = jnp.where(has_next, nxt, cur) * blk_words - tb

    last = tb // N_SLOTS - 1

    def row_of(ref, next_ref, r, u, reach):
        row = ref[pl.ds(jnp.minimum(r, tb - 1), 1), :]
        over = u + reach - N_SLOTS
        if over >= 0:
            row = jnp.where(r < tb, row, next_ref[over:over + 1, :])
        return row

    def group_of_tokens(j, carry):
        @pl.when(jnp.logical_and(j == last, has_next))
        def _():
            idx_copy(i + 1, nxt).wait()

        for u in range(N_SLOTS):
            t = j * N_SLOTS + u
            ta = t + ahead
            word0 = jnp.where(ta < tb, cur * blk_words, next_base) + ta
            fill = (u + ahead) % N_SLOTS

            def start_copy(k, word0=word0, fill=fill):
                slab_copy(word0 + k * tb, k, fill).start(priority=k % DMA_THREADS)

            wait((u + 1) % N_SLOTS)
            weights(g_ref[pl.ds(t, 1), :], parts[u % 2], wbs[u % 2])
            u_side(row_of(h2_ref, h2n_ref, t + 1, u, 1), (u + 1) % N_SLOTS, parts[(u + 1) % 2],
                   start_copy)
            v_side(u, u, wbs[u % 2], start_copy)
        peer_ref[pl.ds(pl.multiple_of(j * N_SLOTS, N_SLOTS), N_SLOTS), :] = stage_ref[...]
        return carry

    lax.fori_loop(0, last + 1, group_of_tokens, 0)

    @pl.when(jnp.logical_not(has_next))
    def _():
        for s in range(1, ahead):
            wait(s)

    y = xl_ref[...] + gt2_ref[0] * peer_ref[...]
    o_ref[...] = _rms(y, gf_ref[...])


def _peer(idx_t, uv, g, h2, xl, gt2, gf, blocks_per_batch):
    nblk, npick, tb = idx_t.shape
    n, d = h2.shape
    nc = d // LANES
    assert tb % N_SLOTS == 0 and N_SLOTS % 8 == 0 and npick % 8 == 0 and nc % 16 == 0
    assert uv.shape[1:] == (2 * nc, LANES) and uv.dtype == BF16
    row = lambda i: (i, 0)
    next_rows = lambda i: (jnp.minimum(i + 1, nblk - 1) * (tb // 8), 0)
    return pl.pallas_call(
        _peer_kernel,
        grid=(nblk,),
        in_specs=[pl.BlockSpec(memory_space=pl.ANY), pl.BlockSpec(memory_space=pl.ANY),
                  pl.BlockSpec((tb, npick), row),
                  pl.BlockSpec((tb, d), row), pl.BlockSpec((8, d), next_rows),
                  pl.BlockSpec((tb, d), row),
                  pl.BlockSpec((1, 1, d), lambda i: (i // blocks_per_batch, 0, 0)),
                  pl.BlockSpec((1, d), lambda i: (0, 0))],
        out_specs=pl.BlockSpec((tb, d), row),
        out_shape=jax.ShapeDtypeStruct((n, d), F32),
        scratch_shapes=[pltpu.SMEM((2 * npick * tb,), jnp.int32), pltpu.SemaphoreType.DMA((2,)),
                        pltpu.SemaphoreType.DMA((N_SLOTS,)), pltpu.VMEM((tb, d), F32),
                        pltpu.VMEM((N_SLOTS, d), F32),
                        pltpu.VMEM((npick, npick), F32), pltpu.VMEM((npick, npick), F32),
                        pltpu.VMEM((npick, LANES), F32), pltpu.VMEM((npick, LANES), F32)]
        + [pltpu.VMEM((npick, 2 * nc, LANES), BF16) for _ in range(N_SLOTS)],
        compiler_params=_params("arbitrary"),
        name="peer",
    )(idx_t.reshape(-1), uv, g, h2, h2, xl, gt2, gf)


def _rope_tables(length):
    rows = length // GRID_W
    row = jnp.broadcast_to(jnp.arange(rows)[:, None], (rows, GRID_W)).reshape(-1)
    col = jnp.broadcast_to(jnp.arange(GRID_W)[None, :], (rows, GRID_W)).reshape(-1)
    inv_freq = ROPE_THETA ** (-jnp.arange(ROPE_FREQS, dtype=F32) / ROPE_FREQS)
    ar = row.astype(F32)[:, None] * inv_freq
    ac = col.astype(F32)[:, None] * inv_freq
    cos = jnp.concatenate([jnp.cos(ar), jnp.cos(ar), jnp.cos(ac), jnp.cos(ac)], axis=-1)
    sin = jnp.concatenate([-jnp.sin(ar), jnp.sin(ar), -jnp.sin(ac), jnp.sin(ac)], axis=-1)
    return cos, sin


def kernel(x, c, ctx, c_ctx, w_ada, b_ada, g_norm1, w_in, g_q, g_k, w_fourier, b_fourier,
           w_out, g_norm2, w_query, sub_keys, u_experts, v_experts, g_final):
    b, l, d = x.shape
    assert w_ada.shape[0] == 1, "single-layer configuration only"
    layer = 0
    tb = LANES

    rows = -(-(b + 1) // 8) * 8
    cc = jnp.concatenate([c, c_ctx[None, :], jnp.zeros((rows - b - 1, d), F32)], axis=0)
    mod = _ada(cc, w_ada[layer], b_ada[layer])
    sh1, sc1, gt1, sh2, sc2, gt2 = [m.reshape(b, 1, d) for m in jnp.split(mod[:b], N_MOD, axis=-1)]
    csh1, csc1 = [jnp.broadcast_to(m.reshape(1, 1, d), (b, 1, d))
                  for m in jnp.split(mod[b], N_MOD, axis=-1)[:2]]

    cos, sin = _rope_tables(l)
    g1 = g_norm1[layer].reshape(1, d)
    gq = g_q[layer].reshape(1, HEAD_DIM)
    gk = g_k[layer].reshape(1, HEAD_DIM)
    w_in_bf = w_in[layer].astype(BF16)
    q, k_l, v_l, f = _inproj(x, g1, sh1, sc1, w_in_bf, gq, gk, cos, sin, latent=True)
    lc = ctx.shape[1]
    k_c, v_c = _inproj(ctx, g1, csh1, csc1, w_in_bf, gq, gk, cos[:lc], sin[:lc], latent=False)

    attn = _attention(q, jnp.concatenate([k_l, k_c], axis=1), jnp.concatenate([v_l, v_c], axis=1))

    dft, chan = _dft_tables(l, f.shape[2] // F_GROUPS)
    four = _fourier(f, _fmat(chan, w_fourier[layer]), dft, b_fourier[layer])

    xl, h2, qp = _outproj(attn, four, x, w_out[layer].astype(BF16), gt1,
                          g_norm2[layer].reshape(1, d), sh2, sc2, w_query[layer].astype(BF16))

    g, idx_t = _retrieve(qp, sub_keys[layer].astype(BF16), tb)
    n_exp = u_experts.shape[1]
    uv = jnp.concatenate([u_experts[layer].astype(BF16).reshape(n_exp, d // LANES, LANES),
                          v_experts[layer].astype(BF16).reshape(n_exp, d // LANES, LANES)], axis=1)
    out = _peer(idx_t, uv, g, h2.reshape(b * l, d), xl.reshape(b * l, d), gt2,
                g_final.reshape(1, d), l // tb)
    return out.reshape(b, l, d)
```

```python
import functools
import math

import numpy as np
import jax
import jax.numpy as jnp
from jax import lax
from jax.experimental import pallas as pl
from jax.experimental.pallas import tpu as pltpu

GRID_W = 64
HEAD_DIM = 128
N_Q_HEADS = 8
N_KV_HEADS = 2
GQA_GROUP = N_Q_HEADS // N_KV_HEADS
ATTN_WIDTH = N_Q_HEADS * HEAD_DIM
KV_WIDTH = N_KV_HEADS * HEAD_DIM
ROPE_THETA = 10000.0
ROPE_FREQS = HEAD_DIM // 4
F_GROUPS = 4
PEER_HEADS = 8
TOPK = 16
N_MOD = 6
EPS = 1e-6

LANES = 128
VMEM_LIMIT = 56 * 1024 * 1024

F32 = jnp.float32
BF16 = jnp.bfloat16
NT_DIMS = (((1,), (1,)), ((), ()))


def _params(*sem):
    return pltpu.CompilerParams(dimension_semantics=sem, vmem_limit_bytes=VMEM_LIMIT)


def _resident(shape, index_map):
    return pl.BlockSpec(shape, index_map, pipeline_mode=pl.Buffered(1))


def _rms(x, g):
    return x * lax.rsqrt(jnp.mean(x * x, axis=-1, keepdims=True) + EPS) * g


def _ada_kernel(c_ref, w_ref, b_ref, o_ref):
    c = c_ref[...]
    a = (c * jax.nn.sigmoid(c)).astype(BF16)
    o_ref[...] = jnp.dot(a, w_ref[...].astype(BF16), preferred_element_type=F32) + b_ref[...]


def _ada(cc, w, b):
    rows, d = cc.shape
    n = w.shape[1]
    tn = next(t for t in (1024, 512, 256, LANES) if n % t == 0)
    return pl.pallas_call(
        _ada_kernel,
        grid=(n // tn,),
        in_specs=[pl.BlockSpec((rows, d), lambda j: (0, 0)),
                  pl.BlockSpec((d, tn), lambda j: (0, j)),
                  pl.BlockSpec((1, tn), lambda j: (0, j))],
        out_specs=pl.BlockSpec((rows, tn), lambda j: (0, j)),
        out_shape=jax.ShapeDtypeStruct((rows, n), F32),
        compiler_params=_params("parallel"),
        name="ada",
    )(cc, w, b.reshape(1, n))


def _rope(t, cos, sin_signed):
    lane = lax.broadcasted_iota(jnp.int32, t.shape, 1)
    partner = jnp.where((lane % 64) < 32, pltpu.roll(t, 96, 1), pltpu.roll(t, 32, 1))
    return t * cos + partner * sin_signed


def _inproj_kernel(x_ref, g1_ref, sh_ref, sc_ref, w_ref, gq_ref, gk_ref, cos_ref, sin_ref,
                   *out_refs, latent):
    h = (_rms(x_ref[0], g1_ref[...]) * (1.0 + sc_ref[0]) + sh_ref[0]).astype(BF16)
    if latent:
        q_ref, k_ref, v_ref, f_ref = out_refs
        cos, sin = cos_ref[...], sin_ref[...]
        q = jnp.dot(h, w_ref[:, 0:ATTN_WIDTH], preferred_element_type=F32)
        scale = HEAD_DIM ** -0.5
        for j in range(N_Q_HEADS):
            sl = slice(j * HEAD_DIM, (j + 1) * HEAD_DIM)
            q_ref[0, :, sl] = (_rope(_rms(q[:, sl], gq_ref[...]), cos, sin) * scale).astype(BF16)
        k0 = ATTN_WIDTH
    else:
        k_ref, v_ref = out_refs
        k0 = 0
    k = jnp.dot(h, w_ref[:, k0:k0 + KV_WIDTH], preferred_element_type=F32)
    for j in range(N_KV_HEADS):
        sl = slice(j * HEAD_DIM, (j + 1) * HEAD_DIM)
        t = _rms(k[:, sl], gk_ref[...])
        if latent:
            t = _rope(t, cos, sin)
        k_ref[0, :, sl] = t.astype(BF16)
    v_ref[0] = jnp.dot(h, w_ref[:, k0 + KV_WIDTH:k0 + 2 * KV_WIDTH],
                       preferred_element_type=F32).astype(BF16)
    if latent:
        f_ref[0] = jnp.dot(h, w_ref[:, ATTN_WIDTH + 2 * KV_WIDTH:],
                           preferred_element_type=F32).astype(BF16)


def _inproj(x, g1, sh, sc, w_bf, gq, gk, cos, sin, latent):
    b, l, d = x.shape
    tm = min(l, 512)
    wcols = w_bf.shape[1]
    fw = wcols - ATTN_WIDTH - 2 * KV_WIDTH
    row = lambda bi, i: (bi, i, 0)
    per_b = pl.BlockSpec((1, 1, d), lambda bi, i: (bi, 0, 0))
    vec = lambda n: pl.BlockSpec((1, n), lambda bi, i: (0, 0))
    if latent:
        w_spec = _resident((d, wcols), lambda bi, i: (0, 0))
        widths = (ATTN_WIDTH, KV_WIDTH, KV_WIDTH, fw)
    else:
        assert ATTN_WIDTH % (2 * KV_WIDTH) == 0
        w_spec = _resident((d, 2 * KV_WIDTH), lambda bi, i: (0, ATTN_WIDTH // (2 * KV_WIDTH)))
        widths = (KV_WIDTH, KV_WIDTH)
    return pl.pallas_call(
        functools.partial(_inproj_kernel, latent=latent),
        grid=(b, l // tm),
        in_specs=[pl.BlockSpec((1, tm, d), row), vec(d), per_b, per_b, w_spec,
                  vec(HEAD_DIM), vec(HEAD_DIM),
                  pl.BlockSpec((tm, HEAD_DIM), lambda bi, i: (i, 0)),
                  pl.BlockSpec((tm, HEAD_DIM), lambda bi, i: (i, 0))],
        out_specs=[pl.BlockSpec((1, tm, n), row) for n in widths],
        out_shape=[jax.ShapeDtypeStruct((b, l, n), BF16) for n in widths],
        compiler_params=_params("parallel", "parallel"),
        name="inproj_latent" if latent else "inproj_ctx",
    )(x, g1, sh, sc, w_bf, gq, gk, cos, sin)


def _attn_kernel(q_ref, k_ref, v_ref, o_ref):
    k = k_ref[0]
    v = v_ref[0]
    for j in range(GQA_GROUP):
        sl = slice(j * HEAD_DIM, (j + 1) * HEAD_DIM)
        s = lax.dot_general(q_ref[0, :, sl], k, NT_DIMS, preferred_element_type=F32)
        p = jnp.exp(s - jnp.max(s, axis=-1, keepdims=True))
        denom = jnp.sum(p, axis=-1, keepdims=True)
        o = jnp.dot(p.astype(BF16), v, preferred_element_type=F32) / denom
        o_ref[0, :, sl] = o.astype(BF16)


def _attention(q, k_all, v_all):
    b, l, _ = q.shape
    lk = k_all.shape[1]
    tq = min(l, 256)
    gw = GQA_GROUP * HEAD_DIM
    return pl.pallas_call(
        _attn_kernel,
        grid=(b, N_KV_HEADS, l // tq),
        in_specs=[pl.BlockSpec((1, tq, gw), lambda bi, g, i: (bi, i, g)),
                  pl.BlockSpec((1, lk, HEAD_DIM), lambda bi, g, i: (bi, 0, g)),
                  pl.BlockSpec((1, lk, HEAD_DIM), lambda bi, g, i: (bi, 0, g))],
        out_specs=pl.BlockSpec((1, tq, gw), lambda bi, g, i: (bi, i, g)),
        out_shape=jax.ShapeDtypeStruct((b, l, ATTN_WIDTH), BF16),
        compiler_params=_params("parallel", "parallel", "parallel"),
        name="attn",
    )(q, k_all, v_all)


def _dft_tables(l, c):
    kl = (np.arange(l, dtype=np.int64)[:, None] * np.arange(l, dtype=np.int64)[None, :]) % l
    ang = 2.0 * np.pi * kl.astype(np.float64) / l
    pos = np.concatenate([np.cos(ang), -np.sin(ang)], axis=1)
    jc = (np.arange(c, dtype=np.int64)[:, None] * np.arange(c, dtype=np.int64)[None, :]) % c
    angc = 2.0 * np.pi * jc.astype(np.float64) / c
    chan = np.stack([np.cos(angc), np.sin(angc)]) / math.sqrt(l * c)

    def split(t):
        head = t.astype(np.float32)
        return jnp.asarray(head) + jnp.asarray((t - head).astype(np.float32))

    return split(pos).astype(BF16), split(chan)


def _fmat_kernel(cs_ref, w_ref, o_ref):
    c = w_ref.shape[-1]
    w = w_ref[0]
    o_ref[0, :, 0:c] = jnp.dot(cs_ref[0], w, preferred_element_type=F32,
                               precision=lax.Precision.HIGHEST).astype(BF16)
    o_ref[0, :, c:2 * c] = jnp.dot(cs_ref[1], w, preferred_element_type=F32,
                                   precision=lax.Precision.HIGHEST).astype(BF16)


def _fmat(chan, w_f):
    g, c, _ = w_f.shape
    return pl.pallas_call(
        _fmat_kernel,
        grid=(g,),
        in_specs=[pl.BlockSpec((2, c, c), lambda gi: (0, 0, 0)),
                  pl.BlockSpec((1, c, c), lambda gi: (gi, 0, 0))],
        out_specs=pl.BlockSpec((1, c, 2 * c), lambda gi: (gi, 0, 0)),
        out_shape=jax.ShapeDtypeStruct((g, c, 2 * c), BF16),
        compiler_params=_params("parallel"),
        name="fmat",
    )(chan, w_f)


def _fourier_kernel(x_ref, m_ref, dft_ref, b_ref, o_ref, z_ref, *, tr):
    l, c = x_ref.shape[1], x_ref.shape[2]
    z = jnp.dot(x_ref[0], m_ref[0], preferred_element_type=F32)
    z_ref[0:l, :] = z[:, 0:c].astype(BF16)
    z_ref[l:2 * l, :] = z[:, c:2 * c].astype(BF16)
    for r in range(l // tr):
        rows = slice(r * tr, (r + 1) * tr)
        y = jnp.dot(dft_ref[rows, :], z_ref[...], preferred_element_type=F32) + b_ref[0]
        o_ref[0, rows, :] = y.astype(BF16)


def _fourier(f, fm, dft, b_f):
    b, l, fw = f.shape
    c = fw // F_GROUPS
    tr = min(l, 512)
    return pl.pallas_call(
        functools.partial(_fourier_kernel, tr=tr),
        grid=(b, F_GROUPS),
        in_specs=[pl.BlockSpec((1, l, c), lambda bi, g: (bi, 0, g)),
                  pl.BlockSpec((1, c, 2 * c), lambda bi, g: (g, 0, 0)),
                  _resident((l, 2 * l), lambda bi, g: (0, 0)),
                  pl.BlockSpec((1, 1, c), lambda bi, g: (g, 0, 0))],
        out_specs=pl.BlockSpec((1, l, c), lambda bi, g: (bi, 0, g)),
        out_shape=jax.ShapeDtypeStruct((b, l, fw), BF16),
        scratch_shapes=[pltpu.VMEM((2 * l, c), BF16)],
        compiler_params=_params("parallel", "parallel"),
        name="fourier",
    )(f, fm, dft, b_f.reshape(F_GROUPS, 1, c))


def _outproj_kernel(a_ref, f_ref, x_ref, wo_ref, gt1_ref, g2_ref, sh2_ref, sc2_ref, wq_ref,
                    xl_ref, h2_ref, qp_ref):
    aw = a_ref.shape[2]
    y = jnp.dot(a_ref[0], wo_ref[0:aw, :], preferred_element_type=F32)
    y = y + jnp.dot(f_ref[0], wo_ref[aw:, :], preferred_element_type=F32)
    xl = x_ref[0] + gt1_ref[0] * y
    xl_ref[0] = xl
    h2 = _rms(xl, g2_ref[...]) * (1.0 + sc2_ref[0]) + sh2_ref[0]
    h2_ref[0] = h2
    qp = jnp.dot(h2.astype(BF16), wq_ref[...], preferred_element_type=F32)
    for j in range(qp_ref.shape[0]):
        qp_ref[j] = qp[:, j * LANES:(j + 1) * LANES].astype(BF16)


def _outproj(attn, four, x, wo_bf, gt1, g2, sh2, sc2, wq_bf):
    b, l, d = x.shape
    tm = min(l, 256)
    nt = l // tm
    aw, fw = attn.shape[2], four.shape[2]
    qw = wq_bf.shape[1]
    row = lambda bi, i: (bi, i, 0)
    per_b = pl.BlockSpec((1, 1, d), lambda bi, i: (bi, 0, 0))
    return pl.pallas_call(
        _outproj_kernel,
        grid=(b, nt),
        in_specs=[pl.BlockSpec((1, tm, aw), row), pl.BlockSpec((1, tm, fw), row),
                  pl.BlockSpec((1, tm, d), row),
                  _resident((aw + fw, d), lambda bi, i: (0, 0)),
                  per_b, pl.BlockSpec((1, d), lambda bi, i: (0, 0)), per_b, per_b,
                  _resident((d, qw), lambda bi, i: (0, 0))],
        out_specs=[pl.BlockSpec((1, tm, d), row), pl.BlockSpec((1, tm, d), row),
                   pl.BlockSpec((qw // LANES, tm, LANES), lambda bi, i: (0, bi * nt + i, 0))],
        out_shape=[jax.ShapeDtypeStruct((b, l, d), F32), jax.ShapeDtypeStruct((b, l, d), F32),
                   jax.ShapeDtypeStruct((qw // LANES, b * l, LANES), BF16)],
        compiler_params=_params("parallel", "parallel"),
        name="outproj",
    )(attn, four, x, wo_bf, gt1, g2, sh2, sc2, wq_bf)


def _topk_rows(s, k, payload=None):
    rows = s.shape[0]
    iota = lax.broadcasted_iota(jnp.int32, s.shape, 0).astype(F32)
    vals, sel = [], []
    for _ in range(k):
        m = jnp.max(s, axis=0, keepdims=True)
        i = jnp.min(jnp.where(s == m, iota, float(rows)), axis=0, keepdims=True)
        hit = iota == i
        vals.append(m)
        sel.append(i if payload is None else
                   jnp.sum(jnp.where(hit, payload, 0.0), axis=0, keepdims=True))
        s = jnp.where(hit, -jnp.inf, s)
    return jnp.concatenate(vals, axis=0), jnp.concatenate(sel, axis=0)


def _pair_candidates(s1, i1, s2, i2, n_keys):
    assert TOPK == 16
    sub = lax.broadcasted_iota(jnp.int32, (8, s1.shape[1]), 0)
    ids = [i1[0:1] * n_keys + i2, i1[1:2] * n_keys + i2[0:8]]
    vals = [s1[0:1] + s2, s1[1:2] + s2[0:8]]
    for a in range(2, 8):
        keep = sub < TOPK // (a + 1)
        vals.append(jnp.where(keep, s1[a:a + 1] + s2[0:8], -jnp.inf))
        ids.append(i1[a:a + 1] * n_keys + i2[0:8])
    vals.append(s1[8:TOPK] + s2[0:1])
    ids.append(i1[8:TOPK] * n_keys + i2[0:1])
    return jnp.concatenate(vals, axis=0), jnp.concatenate(ids, axis=0)


def _retrieve_kernel(q_ref, keys_ref, g_ref, idx_ref, gs_ref, is_ref):
    n_keys = keys_ref.shape[2]

    tokens = q_ref.shape[1]

    def head_pair(hp, carry):
        cands, cidxs = [], []
        for hh in range(2):
            h = 2 * hp + hh
            s = jnp.concatenate(
                [lax.dot_general(keys_ref[h, p], q_ref[2 * h + p], NT_DIMS,
                                 preferred_element_type=F32) for p in range(2)],
                axis=1)
            sv, si = _topk_rows(s, TOPK)
            cand, cidx = _pair_candidates(sv[:, :tokens], si[:, :tokens],
                                          sv[:, tokens:], si[:, tokens:], float(n_keys))
            cands.append(cand)
            cidxs.append(cidx)
        sf, eidx = _topk_rows(jnp.concatenate(cands, axis=1), TOPK,
                              payload=jnp.concatenate(cidxs, axis=1))
        e = jnp.exp(sf - sf[0:1])
        g = e / jnp.sum(e, axis=0, keepdims=True)
        for hh in range(2):
            rows = pl.ds(pl.multiple_of((2 * hp + hh) * TOPK, TOPK), TOPK)
            lanes = slice(hh * tokens, (hh + 1) * tokens)
            gs_ref[rows, :] = g[:, lanes]
            is_ref[rows, :] = eidx[:, lanes].astype(jnp.int32)
        return carry

    assert PEER_HEADS % 2 == 0
    lax.fori_loop(0, PEER_HEADS // 2, head_pair, 0)
    g_ref[...] = gs_ref[...].T
    idx_ref[0] = is_ref[...]


def _retrieve(qp, keys_bf, tb):
    nq, n, _ = qp.shape
    npick = PEER_HEADS * TOPK
    assert tb == LANES and npick == LANES
    nblk = n // tb
    return pl.pallas_call(
        _retrieve_kernel,
        grid=(nblk,),
        in_specs=[pl.BlockSpec((nq, tb, LANES), lambda i: (0, i, 0)),
                  pl.BlockSpec(keys_bf.shape, lambda i: (0, 0, 0, 0))],
        out_specs=[pl.BlockSpec((tb, npick), lambda i: (i, 0)),
                   pl.BlockSpec((1, npick, tb), lambda i: (i, 0, 0))],
        out_shape=[jax.ShapeDtypeStruct((n, npick), F32),
                   jax.ShapeDtypeStruct((nblk, npick, tb), jnp.int32)],
        scratch_shapes=[pltpu.VMEM((npick, tb), F32), pltpu.VMEM((npick, tb), jnp.int32)],
        compiler_params=_params("parallel"),
        name="retrieve",
    )(qp, keys_bf)


N_SLOTS = 8
BITREV3 = (0, 4, 2, 6, 1, 5, 3, 7)
DMA_THREADS = 2


def _peer_kernel(idx_hbm, uv_hbm, g_ref, h2_ref, h2n_ref, xl_ref, gt2_ref, gf_ref, o_ref,
                 idx_smem, idx_sem, sem, peer_ref, stage_ref, wb0, wb1, part0, part1, *bufs):
    tb, d = h2_ref.shape
    npick = g_ref.shape[1]
    nc = d // LANES
    blk_words = npick * tb
    i = pl.program_id(0)
    nblk = pl.num_programs(0)
    cur = i % 2
    nxt = 1 - cur
    has_next = i + 1 < nblk
    ahead = N_SLOTS - 1

    def idx_copy(blk, half):
        return pltpu.make_async_copy(
            idx_hbm.at[pl.ds(pl.multiple_of(blk * blk_words, blk_words), blk_words)],
            idx_smem.at[pl.ds(pl.multiple_of(half * blk_words, blk_words), blk_words)],
            idx_sem.at[half])

    def slab_copy(word, k, slot):
        return pltpu.make_async_copy(uv_hbm.at[idx_smem[word]], bufs[slot].at[k], sem.at[slot])

    def wait(slot):
        pltpu.make_async_copy(uv_hbm.at[pl.ds(0, npick)], bufs[slot], sem.at[slot]).wait()

    sub = lax.broadcasted_iota(jnp.int32, (8, LANES), 0)

    def fold(a, b, dist):
        m = (sub & dist) == 0
        if 2 * dist == 8:
            return jnp.where(m, a, b) + pltpu.roll(jnp.where(m, b, a), dist, 0)
        return (jnp.where(m, a, b)
                + jnp.where(m, pltpu.roll(a, 8 - dist, 0), pltpu.roll(b, dist, 0)))

    def u_side(xrow, slot, part_ref, start_copy=None):
        x = jnp.concatenate([xrow[:, c * LANES:(c + 1) * LANES] for c in range(nc)], axis=0)

        def partial(k):
            p = bufs[slot][k, 0:nc, :].astype(F32) * x
            acc = p[0:8]
            for r in range(1, nc // 8):
                acc = acc + p[8 * r:8 * r + 8]
            return acc

        for grp in range(npick // 8):
            if start_copy is not None:
                for j in range(4):
                    start_copy(grp * 4 + j)
            ps = [partial(grp * 8 + BITREV3[j]) for j in range(8)]
            q = [fold(ps[2 * j], ps[2 * j + 1], 4) for j in range(4)]
            part_ref[grp * 8:grp * 8 + 8, :] = fold(fold(q[0], q[1], 2), fold(q[2], q[3], 2), 1)

    def weights(grow, part_ref, wb_ref):
        a = jnp.sum(part_ref[...].T, axis=0, keepdims=True)
        gelu = 0.5 * a * (1.0 + lax.erf(a * math.sqrt(0.5)))
        wb_ref[...] = jnp.broadcast_to(grow * gelu, (npick, npick)).T

    def v_side(u, slot, wb_ref, start_copy):
        accs = [None] * 4
        for k in range(npick):
            if k % 2 == 0:
                start_copy(npick // 2 + k // 2)
            term = wb_ref[k:k + 1, :] * bufs[slot][k, nc:2 * nc, :].astype(F32)
            accs[k % 4] = term if accs[k % 4] is None else accs[k % 4] + term
        out = (accs[0] + accs[1]) + (accs[2] + accs[3])

        for c in range(nc):
            stage_ref[u:u + 1, c * LANES:(c + 1) * LANES] = out[c:c + 1, :]

    parts = (part0, part1)
    wbs = (wb0, wb1)

    @pl.when(i == 0)
    def _():
        first = idx_copy(0, 0)
        first.start()
        first.wait()
        for t in range(ahead):
            for k in range(npick):
                slab_copy(k * tb + t, k, t).start()
        wait(0)
        u_side(h2_ref[0:1, :], 0, parts[0])

    @pl.when(has_next)
    def _():
        idx_copy(i + 1, nxt).start()

    next_base = jnp.where(has_next, nxt, cur) * blk_words - tb

    last = tb // N_SLOTS - 1

    def row_of(ref, next_ref, r, u, reach):
        row = ref[pl.ds(jnp.minimum(r, tb - 1), 1), :]
        over = u + reach - N_SLOTS
        if over >= 0:
            row = jnp.where(r < tb, row, next_ref[over:over + 1, :])
        return row

    def group_of_tokens(j, carry):
        @pl.when(jnp.logical_and(j == last, has_next))
        def _():
            idx_copy(i + 1, nxt).wait()

        for u in range(N_SLOTS):
            t = j * N_SLOTS + u
            ta = t + ahead
            word0 = jnp.where(ta < tb, cur * blk_words, next_base) + ta
            fill = (u + ahead) % N_SLOTS

            def start_copy(k, word0=word0, fill=fill):
                slab_copy(word0 + k * tb, k, fill).start(priority=k % DMA_THREADS)

            wait((u + 1) % N_SLOTS)
            weights(g_ref[pl.ds(t, 1), :], parts[u % 2], wbs[u % 2])
            u_side(row_of(h2_ref, h2n_ref, t + 1, u, 1), (u + 1) % N_SLOTS, parts[(u + 1) % 2],
                   start_copy)
            v_side(u, u, wbs[u % 2], start_copy)
        peer_ref[pl.ds(pl.multiple_of(j * N_SLOTS, N_SLOTS), N_SLOTS), :] = stage_ref[...]
        return carry

    lax.fori_loop(0, last + 1, group_of_tokens, 0)

    @pl.when(jnp.logical_not(has_next))
    def _():
        for s in range(1, ahead):
            wait(s)

    y = xl_ref[...] + gt2_ref[0] * peer_ref[...]
    o_ref[...] = _rms(y, gf_ref[...])


def _peer(idx_t, uv, g, h2, xl, gt2, gf, blocks_per_batch):
    nblk, npick, tb = idx_t.shape
    n, d = h2.shape
    nc = d // LANES
    assert tb % N_SLOTS == 0 and N_SLOTS % 8 == 0 and npick % 8 == 0 and nc % 16 == 0
    assert uv.shape[1:] == (2 * nc, LANES) and uv.dtype == BF16
    row = lambda i: (i, 0)
    next_rows = lambda i: (jnp.minimum(i + 1, nblk - 1) * (tb // 8), 0)
    return pl.pallas_call(
        _peer_kernel,
        grid=(nblk,),
        in_specs=[pl.BlockSpec(memory_space=pl.ANY), pl.BlockSpec(memory_space=pl.ANY),
                  pl.BlockSpec((tb, npick), row),
                  pl.BlockSpec((tb, d), row), pl.BlockSpec((8, d), next_rows),
                  pl.BlockSpec((tb, d), row),
                  pl.BlockSpec((1, 1, d), lambda i: (i // blocks_per_batch, 0, 0)),
                  pl.BlockSpec((1, d), lambda i: (0, 0))],
        out_specs=pl.BlockSpec((tb, d), row),
        out_shape=jax.ShapeDtypeStruct((n, d), F32),
        scratch_shapes=[pltpu.SMEM((2 * npick * tb,), jnp.int32), pltpu.SemaphoreType.DMA((2,)),
                        pltpu.SemaphoreType.DMA((N_SLOTS,)), pltpu.VMEM((tb, d), F32),
                        pltpu.VMEM((N_SLOTS, d), F32),
                        pltpu.VMEM((npick, npick), F32), pltpu.VMEM((npick, npick), F32),
                        pltpu.VMEM((npick, LANES), F32), pltpu.VMEM((npick, LANES), F32)]
        + [pltpu.VMEM((npick, 2 * nc, LANES), BF16) for _ in range(N_SLOTS)],
        compiler_params=_params("arbitrary"),
        name="peer",
    )(idx_t.reshape(-1), uv, g, h2, h2, xl, gt2, gf)


def _rope_tables(length):
    rows = length // GRID_W
    row = jnp.broadcast_to(jnp.arange(rows)[:, None], (rows, GRID_W)).reshape(-1)
    col = jnp.broadcast_to(jnp.arange(GRID_W)[None, :], (rows, GRID_W)).reshape(-1)
    inv_freq = ROPE_THETA ** (-jnp.arange(ROPE_FREQS, dtype=F32) / ROPE_FREQS)
    ar = row.astype(F32)[:, None] * inv_freq
    ac = col.astype(F32)[:, None] * inv_freq
    cos = jnp.concatenate([jnp.cos(ar), jnp.cos(ar), jnp.cos(ac), jnp.cos(ac)], axis=-1)
    sin = jnp.concatenate([-jnp.sin(ar), jnp.sin(ar), -jnp.sin(ac), jnp.sin(ac)], axis=-1)
    return cos, sin


def kernel(x, c, ctx, c_ctx, w_ada, b_ada, g_norm1, w_in, g_q, g_k, w_fourier, b_fourier,
           w_out, g_norm2, w_query, sub_keys, u_experts, v_experts, g_final):
    b, l, d = x.shape
    assert w_ada.shape[0] == 1, "single-layer configuration only"
    layer = 0
    tb = LANES

    rows = -(-(b + 1) // 8) * 8
    cc = jnp.concatenate([c, c_ctx[None, :], jnp.zeros((rows - b - 1, d), F32)], axis=0)
    mod = _ada(cc, w_ada[layer], b_ada[layer])
    sh1, sc1, gt1, sh2, sc2, gt2 = [m.reshape(b, 1, d) for m in jnp.split(mod[:b], N_MOD, axis=-1)]
    csh1, csc1 = [jnp.broadcast_to(m.reshape(1, 1, d), (b, 1, d))
                  for m in jnp.split(mod[b], N_MOD, axis=-1)[:2]]

    cos, sin = _rope_tables(l)
    g1 = g_norm1[layer].reshape(1, d)
    gq = g_q[layer].reshape(1, HEAD_DIM)
    gk = g_k[layer].reshape(1, HEAD_DIM)
    w_in_bf = w_in[layer].astype(BF16)
    q, k_l, v_l, f = _inproj(x, g1, sh1, sc1, w_in_bf, gq, gk, cos, sin, latent=True)
    lc = ctx.shape[1]
    k_c, v_c = _inproj(ctx, g1, csh1, csc1, w_in_bf, gq, gk, cos[:lc], sin[:lc], latent=False)

    attn = _attention(q, jnp.concatenate([k_l, k_c], axis=1), jnp.concatenate([v_l, v_c], axis=1))

    dft, chan = _dft_tables(l, f.shape[2] // F_GROUPS)
    four = _fourier(f, _fmat(chan, w_fourier[layer]), dft, b_fourier[layer])

    xl, h2, qp = _outproj(attn, four, x, w_out[layer].astype(BF16), gt1,
                          g_norm2[layer].reshape(1, d), sh2, sc2, w_query[layer].astype(BF16))

    g, idx_t = _retrieve(qp, sub_keys[layer].astype(BF16), tb)
    n_exp = u_experts.shape[1]
    uv = jnp.stack([u_experts[layer].astype(BF16).reshape(n_exp, d // LANES, LANES),
                    v_experts[layer].astype(BF16).reshape(n_exp, d // LANES, LANES)],
                   axis=1).reshape(n_exp, 2 * (d // LANES), LANES)
    out = _peer(idx_t, uv, g, h2.reshape(b * l, d), xl.reshape(b * l, d), gt2,
                g_final.reshape(1, d), l // tb)
    return out.reshape(b, l, d)
```

```python
import functools
import math

import numpy as np
import jax
import jax.numpy as jnp
from jax import lax
from jax.experimental import pallas as pl
from jax.experimental.pallas import tpu as pltpu

GRID_W = 64
HEAD_DIM = 128
N_Q_HEADS = 8
N_KV_HEADS = 2
GQA_GROUP = N_Q_HEADS // N_KV_HEADS
ATTN_WIDTH = N_Q_HEADS * HEAD_DIM
KV_WIDTH = N_KV_HEADS * HEAD_DIM
ROPE_THETA = 10000.0
ROPE_FREQS = HEAD_DIM // 4
F_GROUPS = 4
PEER_HEADS = 8
TOPK = 16
N_MOD = 6
EPS = 1e-6

LANES = 128
VMEM_LIMIT = 56 * 1024 * 1024

F32 = jnp.float32
BF16 = jnp.bfloat16
NT_DIMS = (((1,), (1,)), ((), ()))


def _params(*sem):
    return pltpu.CompilerParams(dimension_semantics=sem, vmem_limit_bytes=VMEM_LIMIT)


def _resident(shape, index_map):
    return pl.BlockSpec(shape, index_map, pipeline_mode=pl.Buffered(1))


def _rms(x, g):
    return x * lax.rsqrt(jnp.mean(x * x, axis=-1, keepdims=True) + EPS) * g


def _ada_kernel(c_ref, w_ref, b_ref, o_ref):
    c = c_ref[...]
    a = (c * jax.nn.sigmoid(c)).astype(BF16)
    o_ref[...] = jnp.dot(a, w_ref[...].astype(BF16), preferred_element_type=F32) + b_ref[...]


def _ada(cc, w, b):
    rows, d = cc.shape
    n = w.shape[1]
    tn = next(t for t in (1024, 512, 256, LANES) if n % t == 0)
    return pl.pallas_call(
        _ada_kernel,
        grid=(n // tn,),
        in_specs=[pl.BlockSpec((rows, d), lambda j: (0, 0)),
                  pl.BlockSpec((d, tn), lambda j: (0, j)),
                  pl.BlockSpec((1, tn), lambda j: (0, j))],
        out_specs=pl.BlockSpec((rows, tn), lambda j: (0, j)),
        out_shape=jax.ShapeDtypeStruct((rows, n), F32),
        compiler_params=_params("parallel"),
        name="ada",
    )(cc, w, b.reshape(1, n))


def _rope(t, cos, sin_signed):
    lane = lax.broadcasted_iota(jnp.int32, t.shape, 1)
    partner = jnp.where((lane % 64) < 32, pltpu.roll(t, 96, 1), pltpu.roll(t, 32, 1))
    return t * cos + partner * sin_signed


def _inproj_kernel(x_ref, g1_ref, sh_ref, sc_ref, w_ref, gq_ref, gk_ref, cos_ref, sin_ref,
                   *out_refs, latent):
    h = (_rms(x_ref[0], g1_ref[...]) * (1.0 + sc_ref[0]) + sh_ref[0]).astype(BF16)
    if latent:
        q_ref, k_ref, v_ref, f_ref = out_refs
        cos, sin = cos_ref[...], sin_ref[...]
        q = jnp.dot(h, w_ref[:, 0:ATTN_WIDTH], preferred_element_type=F32)
        scale = HEAD_DIM ** -0.5
        for j in range(N_Q_HEADS):
            sl = slice(j * HEAD_DIM, (j + 1) * HEAD_DIM)
            q_ref[0, :, sl] = (_rope(_rms(q[:, sl], gq_ref[...]), cos, sin) * scale).astype(BF16)
        k0 = ATTN_WIDTH
    else:
        k_ref, v_ref = out_refs
        k0 = 0
    k = jnp.dot(h, w_ref[:, k0:k0 + KV_WIDTH], preferred_element_type=F32)
    for j in range(N_KV_HEADS):
        sl = slice(j * HEAD_DIM, (j + 1) * HEAD_DIM)
        t = _rms(k[:, sl], gk_ref[...])
        if latent:
            t = _rope(t, cos, sin)
        k_ref[0, :, sl] = t.astype(BF16)
    v_ref[0] = jnp.dot(h, w_ref[:, k0 + KV_WIDTH:k0 + 2 * KV_WIDTH],
                       preferred_element_type=F32).astype(BF16)
    if latent:
        f_ref[0] = jnp.dot(h, w_ref[:, ATTN_WIDTH + 2 * KV_WIDTH:],
                           preferred_element_type=F32).astype(BF16)


def _inproj(x, g1, sh, sc, w_bf, gq, gk, cos, sin, latent):
    b, l, d = x.shape
    tm = min(l, 512)
    wcols = w_bf.shape[1]
    fw = wcols - ATTN_WIDTH - 2 * KV_WIDTH
    row = lambda bi, i: (bi, i, 0)
    per_b = pl.BlockSpec((1, 1, d), lambda bi, i: (bi, 0, 0))
    vec = lambda n: pl.BlockSpec((1, n), lambda bi, i: (0, 0))
    if latent:
        w_spec = _resident((d, wcols), lambda bi, i: (0, 0))
        widths = (ATTN_WIDTH, KV_WIDTH, KV_WIDTH, fw)
    else:
        assert ATTN_WIDTH % (2 * KV_WIDTH) == 0
        w_spec = _resident((d, 2 * KV_WIDTH), lambda bi, i: (0, ATTN_WIDTH // (2 * KV_WIDTH)))
        widths = (KV_WIDTH, KV_WIDTH)
    return pl.pallas_call(
        functools.partial(_inproj_kernel, latent=latent),
        grid=(b, l // tm),
        in_specs=[pl.BlockSpec((1, tm, d), row), vec(d), per_b, per_b, w_spec,
                  vec(HEAD_DIM), vec(HEAD_DIM),
                  pl.BlockSpec((tm, HEAD_DIM), lambda bi, i: (i, 0)),
                  pl.BlockSpec((tm, HEAD_DIM), lambda bi, i: (i, 0))],
        out_specs=[pl.BlockSpec((1, tm, n), row) for n in widths],
        out_shape=[jax.ShapeDtypeStruct((b, l, n), BF16) for n in widths],
        compiler_params=_params("parallel", "parallel"),
        name="inproj_latent" if latent else "inproj_ctx",
    )(x, g1, sh, sc, w_bf, gq, gk, cos, sin)


def _attn_kernel(q_ref, k_ref, v_ref, o_ref):
    k = k_ref[0]
    v = v_ref[0]
    for j in range(GQA_GROUP):
        sl = slice(j * HEAD_DIM, (j + 1) * HEAD_DIM)
        s = lax.dot_general(q_ref[0, :, sl], k, NT_DIMS, preferred_element_type=F32)
        p = jnp.exp(s - jnp.max(s, axis=-1, keepdims=True))
        denom = jnp.sum(p, axis=-1, keepdims=True)
        o = jnp.dot(p.astype(BF16), v, preferred_element_type=F32) / denom
        o_ref[0, :, sl] = o.astype(BF16)


def _attention(q, k_all, v_all):
    b, l, _ = q.shape
    lk = k_all.shape[1]
    tq = min(l, 256)
    gw = GQA_GROUP * HEAD_DIM
    return pl.pallas_call(
        _attn_kernel,
        grid=(b, N_KV_HEADS, l // tq),
        in_specs=[pl.BlockSpec((1, tq, gw), lambda bi, g, i: (bi, i, g)),
                  pl.BlockSpec((1, lk, HEAD_DIM), lambda bi, g, i: (bi, 0, g)),
                  pl.BlockSpec((1, lk, HEAD_DIM), lambda bi, g, i: (bi, 0, g))],
        out_specs=pl.BlockSpec((1, tq, gw), lambda bi, g, i: (bi, i, g)),
        out_shape=jax.ShapeDtypeStruct((b, l, ATTN_WIDTH), BF16),
        compiler_params=_params("parallel", "parallel", "parallel"),
        name="attn",
    )(q, k_all, v_all)


def _dft_tables(l, c):
    kl = (np.arange(l, dtype=np.int64)[:, None] * np.arange(l, dtype=np.int64)[None, :]) % l
    ang = 2.0 * np.pi * kl.astype(np.float64) / l
    pos = np.concatenate([np.cos(ang), -np.sin(ang)], axis=1)
    jc = (np.arange(c, dtype=np.int64)[:, None] * np.arange(c, dtype=np.int64)[None, :]) % c
    angc = 2.0 * np.pi * jc.astype(np.float64) / c
    chan = np.stack([np.cos(angc), np.sin(angc)]) / math.sqrt(l * c)

    def split(t):
        head = t.astype(np.float32)
        return jnp.asarray(head) + jnp.asarray((t - head).astype(np.float32))

    return split(pos).astype(BF16), split(chan)


def _fmat_kernel(cs_ref, w_ref, o_ref):
    c = w_ref.shape[-1]
    w = w_ref[0]
    o_ref[0, :, 0:c] = jnp.dot(cs_ref[0], w, preferred_element_type=F32,
                               precision=lax.Precision.HIGHEST).astype(BF16)
    o_ref[0, :, c:2 * c] = jnp.dot(cs_ref[1], w, preferred_element_type=F32,
                                   precision=lax.Precision.HIGHEST).astype(BF16)


def _fmat(chan, w_f):
    g, c, _ = w_f.shape
    return pl.pallas_call(
        _fmat_kernel,
        grid=(g,),
        in_specs=[pl.BlockSpec((2, c, c), lambda gi: (0, 0, 0)),
                  pl.BlockSpec((1, c, c), lambda gi: (gi, 0, 0))],
        out_specs=pl.BlockSpec((1, c, 2 * c), lambda gi: (gi, 0, 0)),
        out_shape=jax.ShapeDtypeStruct((g, c, 2 * c), BF16),
        compiler_params=_params("parallel"),
        name="fmat",
    )(chan, w_f)


def _fourier_kernel(x_ref, m_ref, dft_ref, b_ref, o_ref, z_ref, *, tr):
    l, c = x_ref.shape[1], x_ref.shape[2]
    z = jnp.dot(x_ref[0], m_ref[0], preferred_element_type=F32)
    z_ref[0:l, :] = z[:, 0:c].astype(BF16)
    z_ref[l:2 * l, :] = z[:, c:2 * c].astype(BF16)
    for r in range(l // tr):
        rows = slice(r * tr, (r + 1) * tr)
        y = jnp.dot(dft_ref[rows, :], z_ref[...], preferred_element_type=F32) + b_ref[0]
        o_ref[0, rows, :] = y.astype(BF16)


def _fourier(f, fm, dft, b_f):
    b, l, fw = f.shape
    c = fw // F_GROUPS
    tr = min(l, 512)
    return pl.pallas_call(
        functools.partial(_fourier_kernel, tr=tr),
        grid=(b, F_GROUPS),
        in_specs=[pl.BlockSpec((1, l, c), lambda bi, g: (bi, 0, g)),
                  pl.BlockSpec((1, c, 2 * c), lambda bi, g: (g, 0, 0)),
                  _resident((l, 2 * l), lambda bi, g: (0, 0)),
                  pl.BlockSpec((1, 1, c), lambda bi, g: (g, 0, 0))],
        out_specs=pl.BlockSpec((1, l, c), lambda bi, g: (bi, 0, g)),
        out_shape=jax.ShapeDtypeStruct((b, l, fw), BF16),
        scratch_shapes=[pltpu.VMEM((2 * l, c), BF16)],
        compiler_params=_params("parallel", "parallel"),
        name="fourier",
    )(f, fm, dft, b_f.reshape(F_GROUPS, 1, c))


def _outproj_kernel(a_ref, f_ref, x_ref, wo_ref, gt1_ref, g2_ref, sh2_ref, sc2_ref, wq_ref,
                    xl_ref, h2_ref, qp_ref):
    aw = a_ref.shape[2]
    y = jnp.dot(a_ref[0], wo_ref[0:aw, :], preferred_element_type=F32)
    y = y + jnp.dot(f_ref[0], wo_ref[aw:, :], preferred_element_type=F32)
    xl = x_ref[0] + gt1_ref[0] * y
    xl_ref[0] = xl
    h2 = _rms(xl, g2_ref[...]) * (1.0 + sc2_ref[0]) + sh2_ref[0]
    h2_ref[0] = h2
    qp = jnp.dot(h2.astype(BF16), wq_ref[...], preferred_element_type=F32)
    for j in range(qp_ref.shape[0]):
        qp_ref[j] = qp[:, j * LANES:(j + 1) * LANES].astype(BF16)


def _outproj(attn, four, x, wo_bf, gt1, g2, sh2, sc2, wq_bf):
    b, l, d = x.shape
    tm = min(l, 256)
    nt = l // tm
    aw, fw = attn.shape[2], four.shape[2]
    qw = wq_bf.shape[1]
    row = lambda bi, i: (bi, i, 0)
    per_b = pl.BlockSpec((1, 1, d), lambda bi, i: (bi, 0, 0))
    return pl.pallas_call(
        _outproj_kernel,
        grid=(b, nt),
        in_specs=[pl.BlockSpec((1, tm, aw), row), pl.BlockSpec((1, tm, fw), row),
                  pl.BlockSpec((1, tm, d), row),
                  _resident((aw + fw, d), lambda bi, i: (0, 0)),
                  per_b, pl.BlockSpec((1, d), lambda bi, i: (0, 0)), per_b, per_b,
                  _resident((d, qw), lambda bi, i: (0, 0))],
        out_specs=[pl.BlockSpec((1, tm, d), row), pl.BlockSpec((1, tm, d), row),
                   pl.BlockSpec((qw // LANES, tm, LANES), lambda bi, i: (0, bi * nt + i, 0))],
        out_shape=[jax.ShapeDtypeStruct((b, l, d), F32), jax.ShapeDtypeStruct((b, l, d), F32),
                   jax.ShapeDtypeStruct((qw // LANES, b * l, LANES), BF16)],
        compiler_params=_params("parallel", "parallel"),
        name="outproj",
    )(attn, four, x, wo_bf, gt1, g2, sh2, sc2, wq_bf)


def _topk_rows(s, k, payload=None):
    rows = s.shape[0]
    iota = lax.broadcasted_iota(jnp.int32, s.shape, 0).astype(F32)
    vals, sel = [], []
    for _ in range(k):
        m = jnp.max(s, axis=0, keepdims=True)
        i = jnp.min(jnp.where(s == m, iota, float(rows)), axis=0, keepdims=True)
        hit = iota == i
        vals.append(m)
        sel.append(i if payload is None else
                   jnp.sum(jnp.where(hit, payload, 0.0), axis=0, keepdims=True))
        s = jnp.where(hit, -jnp.inf, s)
    return jnp.concatenate(vals, axis=0), jnp.concatenate(sel, axis=0)


def _pair_candidates(s1, i1, s2, i2, n_keys):
    assert TOPK == 16
    sub = lax.broadcasted_iota(jnp.int32, (8, s1.shape[1]), 0)
    ids = [i1[0:1] * n_keys + i2, i1[1:2] * n_keys + i2[0:8]]
    vals = [s1[0:1] + s2, s1[1:2] + s2[0:8]]
    for a in range(2, 8):
        keep = sub < TOPK // (a + 1)
        vals.append(jnp.where(keep, s1[a:a + 1] + s2[0:8], -jnp.inf))
        ids.append(i1[a:a + 1] * n_keys + i2[0:8])
    vals.append(s1[8:TOPK] + s2[0:1])
    ids.append(i1[8:TOPK] * n_keys + i2[0:1])
    return jnp.concatenate(vals, axis=0), jnp.concatenate(ids, axis=0)


def _retrieve_kernel(q_ref, keys_ref, g_ref, idx_ref, gs_ref, is_ref):
    n_keys = keys_ref.shape[2]

    tokens = q_ref.shape[1]

    def head_pair(hp, carry):
        cands, cidxs = [], []
        for hh in range(2):
            h = 2 * hp + hh
            s = jnp.concatenate(
                [lax.dot_general(keys_ref[h, p], q_ref[2 * h + p], NT_DIMS,
                                 preferred_element_type=F32) for p in range(2)],
                axis=1)
            sv, si = _topk_rows(s, TOPK)
            cand, cidx = _pair_candidates(sv[:, :tokens], si[:, :tokens],
                                          sv[:, tokens:], si[:, tokens:], float(n_keys))
            cands.append(cand)
            cidxs.append(cidx)
        sf, eidx = _topk_rows(jnp.concatenate(cands, axis=1), TOPK,
                              payload=jnp.concatenate(cidxs, axis=1))
        e = jnp.exp(sf - sf[0:1])
        g = e / jnp.sum(e, axis=0, keepdims=True)
        for hh in range(2):
            rows = pl.ds(pl.multiple_of((2 * hp + hh) * TOPK, TOPK), TOPK)
            lanes = slice(hh * tokens, (hh + 1) * tokens)
            gs_ref[rows, :] = g[:, lanes]
            is_ref[rows, :] = eidx[:, lanes].astype(jnp.int32)
        return carry

    assert PEER_HEADS % 2 == 0
    lax.fori_loop(0, PEER_HEADS // 2, head_pair, 0)
    g_ref[...] = gs_ref[...].T
    idx_ref[0] = is_ref[...]


def _retrieve(qp, keys_bf, tb):
    nq, n, _ = qp.shape
    npick = PEER_HEADS * TOPK
    assert tb == LANES and npick == LANES
    nblk = n // tb
    return pl.pallas_call(
        _retrieve_kernel,
        grid=(nblk,),
        in_specs=[pl.BlockSpec((nq, tb, LANES), lambda i: (0, i, 0)),
                  pl.BlockSpec(keys_bf.shape, lambda i: (0, 0, 0, 0))],
        out_specs=[pl.BlockSpec((tb, npick), lambda i: (i, 0)),
                   pl.BlockSpec((1, npick, tb), lambda i: (i, 0, 0))],
        out_shape=[jax.ShapeDtypeStruct((n, npick), F32),
                   jax.ShapeDtypeStruct((nblk, npick, tb), jnp.int32)],
        scratch_shapes=[pltpu.VMEM((npick, tb), F32), pltpu.VMEM((npick, tb), jnp.int32)],
        compiler_params=_params("parallel"),
        name="retrieve",
    )(qp, keys_bf)


N_SLOTS = 8
BITREV3 = (0, 4, 2, 6, 1, 5, 3, 7)
DMA_THREADS = 2


def _peer_kernel(idx_hbm, uv_hbm, g_ref, h2_ref, h2n_ref, xl_ref, gt2_ref, gf_ref, o_ref,
                 idx_smem, idx_sem, sem, peer_ref, stage_ref, wb0, wb1, part0, part1, *bufs):
    tb, d = h2_ref.shape
    npick = g_ref.shape[1]
    nc = d // LANES
    blk_words = npick * tb
    i = pl.program_id(0)
    nblk = pl.num_programs(0)
    cur = i % 2
    nxt = 1 - cur
    has_next = i + 1 < nblk
    ahead = N_SLOTS - 1

    def idx_copy(blk, half):
        return pltpu.make_async_copy(
            idx_hbm.at[pl.ds(pl.multiple_of(blk * blk_words, blk_words), blk_words)],
            idx_smem.at[pl.ds(pl.multiple_of(half * blk_words, blk_words), blk_words)],
            idx_sem.at[half])

    def slab_copy(word, k, slot):
        return pltpu.make_async_copy(uv_hbm.at[idx_smem[word]], bufs[slot].at[k], sem.at[slot])

    def wait(slot):
        pltpu.make_async_copy(uv_hbm.at[pl.ds(0, npick)], bufs[slot], sem.at[slot]).wait()

    sub = lax.broadcasted_iota(jnp.int32, (8, LANES), 0)

    def fold(a, b, dist):
        m = (sub & dist) == 0
        if 2 * dist == 8:
            return jnp.where(m, a, b) + pltpu.roll(jnp.where(m, b, a), dist, 0)
        return (jnp.where(m, a, b)
                + jnp.where(m, pltpu.roll(a, 8 - dist, 0), pltpu.roll(b, dist, 0)))

    def u_side(xrow, slot, part_ref, start_copy=None):
        x = jnp.concatenate([xrow[:, c * LANES:(c + 1) * LANES] for c in range(nc)],
                            axis=0).astype(BF16)

        def partial(k):
            p = (bufs[slot][k, 0:nc, :] * x).astype(F32)
            acc = p[0:8]
            for r in range(1, nc // 8):
                acc = acc + p[8 * r:8 * r + 8]
            return acc

        for grp in range(npick // 8):
            if start_copy is not None:
                for j in range(4):
                    start_copy(grp * 4 + j)
            ps = [partial(grp * 8 + BITREV3[j]) for j in range(8)]
            q = [fold(ps[2 * j], ps[2 * j + 1], 4) for j in range(4)]
            part_ref[grp * 8:grp * 8 + 8, :] = fold(fold(q[0], q[1], 2), fold(q[2], q[3], 2), 1)

    def weights(grow, part_ref, wb_ref):
        a = jnp.sum(part_ref[...].T, axis=0, keepdims=True)
        gelu = 0.5 * a * (1.0 + lax.erf(a * math.sqrt(0.5)))
        wb_ref[...] = jnp.broadcast_to(grow * gelu, (npick, npick)).T

    def v_side(u, slot, wb_ref, start_copy):
        accs = [None] * 4
        for k in range(npick):
            if k % 2 == 0:
                start_copy(npick // 2 + k // 2)
            term = wb_ref[k:k + 1, :] * bufs[slot][k, nc:2 * nc, :].astype(F32)
            accs[k % 4] = term if accs[k % 4] is None else accs[k % 4] + term
        out = (accs[0] + accs[1]) + (accs[2] + accs[3])

        for c in range(nc):
            stage_ref[u:u + 1, c * LANES:(c + 1) * LANES] = out[c:c + 1, :]

    parts = (part0, part1)
    wbs = (wb0, wb1)

    @pl.when(i == 0)
    def _():
        first = idx_copy(0, 0)
        first.start()
        first.wait()
        for t in range(ahead):
            for k in range(npick):
                slab_copy(k * tb + t, k, t).start()
        wait(0)
        u_side(h2_ref[0:1, :], 0, parts[0])

    @pl.when(has_next)
    def _():
        idx_copy(i + 1, nxt).start()

    next_base = jnp.where(has_next, nxt, cur) * blk_words - tb

    last = tb // N_SLOTS - 1

    def row_of(ref, next_ref, r, u, reach):
        row = ref[pl.ds(jnp.minimum(r, tb - 1), 1), :]
        over = u + reach - N_SLOTS
        if over >= 0:
            row = jnp.where(r < tb, row, next_ref[over:over + 1, :])
        return row

    def group_of_tokens(j, carry):
        @pl.when(jnp.logical_and(j == last, has_next))
        def _():
            idx_copy(i + 1, nxt).wait()

        for u in range(N_SLOTS):
            t = j * N_SLOTS + u
            ta = t + ahead
            word0 = jnp.where(ta < tb, cur * blk_words, next_base) + ta
            fill = (u + ahead) % N_SLOTS

            def start_copy(k, word0=word0, fill=fill):
                slab_copy(word0 + k * tb, k, fill).start(priority=k % DMA_THREADS)

            wait((u + 1) % N_SLOTS)
            weights(g_ref[pl.ds(t, 1), :], parts[u % 2], wbs[u % 2])
            u_side(row_of(h2_ref, h2n_ref, t + 1, u, 1), (u + 1) % N_SLOTS, parts[(u + 1) % 2],
                   start_copy)
            v_side(u, u, wbs[u % 2], start_copy)
        peer_ref[pl.ds(pl.multiple_of(j * N_SLOTS, N_SLOTS), N_SLOTS), :] = stage_ref[...]
        return carry

    lax.fori_loop(0, last + 1, group_of_tokens, 0)

    @pl.when(jnp.logical_not(has_next))
    def _():
        for s in range(1, ahead):
            wait(s)

    y = xl_ref[...] + gt2_ref[0] * peer_ref[...]
    o_ref[...] = _rms(y, gf_ref[...])


def _peer(idx_t, uv, g, h2, xl, gt2, gf, blocks_per_batch):
    nblk, npick, tb = idx_t.shape
    n, d = h2.shape
    nc = d // LANES
    assert tb % N_SLOTS == 0 and N_SLOTS % 8 == 0 and npick % 8 == 0 and nc % 16 == 0
    assert uv.shape[1:] == (2 * nc, LANES) and uv.dtype == BF16
    row = lambda i: (i, 0)
    next_rows = lambda i: (jnp.minimum(i + 1, nblk - 1) * (tb // 8), 0)
    return pl.pallas_call(
        _peer_kernel,
        grid=(nblk,),
        in_specs=[pl.BlockSpec(memory_space=pl.ANY), pl.BlockSpec(memory_space=pl.ANY),
                  pl.BlockSpec((tb, npick), row),
                  pl.BlockSpec((tb, d), row), pl.BlockSpec((8, d), next_rows),
                  pl.BlockSpec((tb, d), row),
                  pl.BlockSpec((1, 1, d), lambda i: (i // blocks_per_batch, 0, 0)),
                  pl.BlockSpec((1, d), lambda i: (0, 0))],
        out_specs=pl.BlockSpec((tb, d), row),
        out_shape=jax.ShapeDtypeStruct((n, d), F32),
        scratch_shapes=[pltpu.SMEM((2 * npick * tb,), jnp.int32), pltpu.SemaphoreType.DMA((2,)),
                        pltpu.SemaphoreType.DMA((N_SLOTS,)), pltpu.VMEM((tb, d), F32),
                        pltpu.VMEM((N_SLOTS, d), F32),
                        pltpu.VMEM((npick, npick), F32), pltpu.VMEM((npick, npick), F32),
                        pltpu.VMEM((npick, LANES), F32), pltpu.VMEM((npick, LANES), F32)]
        + [pltpu.VMEM((npick, 2 * nc, LANES), BF16) for _ in range(N_SLOTS)],
        compiler_params=_params("arbitrary"),
        name="peer",
    )(idx_t.reshape(-1), uv, g, h2, h2, xl, gt2, gf)


def _rope_tables(length):
    rows = length // GRID_W
    row = jnp.broadcast_to(jnp.arange(rows)[:, None], (rows, GRID_W)).reshape(-1)
    col = jnp.broadcast_to(jnp.arange(GRID_W)[None, :], (rows, GRID_W)).reshape(-1)
    inv_freq = ROPE_THETA ** (-jnp.arange(ROPE_FREQS, dtype=F32) / ROPE_FREQS)
    ar = row.astype(F32)[:, None] * inv_freq
    ac = col.astype(F32)[:, None] * inv_freq
    cos = jnp.concatenate([jnp.cos(ar), jnp.cos(ar), jnp.cos(ac), jnp.cos(ac)], axis=-1)
    sin = jnp.concatenate([-jnp.sin(ar), jnp.sin(ar), -jnp.sin(ac), jnp.sin(ac)], axis=-1)
    return cos, sin


def kernel(x, c, ctx, c_ctx, w_ada, b_ada, g_norm1, w_in, g_q, g_k, w_fourier, b_fourier,
           w_out, g_norm2, w_query, sub_keys, u_experts, v_experts, g_final):
    b, l, d = x.shape
    assert w_ada.shape[0] == 1, "single-layer configuration only"
    layer = 0
    tb = LANES

    rows = -(-(b + 1) // 8) * 8
    cc = jnp.concatenate([c, c_ctx[None, :], jnp.zeros((rows - b - 1, d), F32)], axis=0)
    mod = _ada(cc, w_ada[layer], b_ada[layer])
    sh1, sc1, gt1, sh2, sc2, gt2 = [m.reshape(b, 1, d) for m in jnp.split(mod[:b], N_MOD, axis=-1)]
    csh1, csc1 = [jnp.broadcast_to(m.reshape(1, 1, d), (b, 1, d))
                  for m in jnp.split(mod[b], N_MOD, axis=-1)[:2]]

    cos, sin = _rope_tables(l)
    g1 = g_norm1[layer].reshape(1, d)
    gq = g_q[layer].reshape(1, HEAD_DIM)
    gk = g_k[layer].reshape(1, HEAD_DIM)
    w_in_bf = w_in[layer].astype(BF16)
    q, k_l, v_l, f = _inproj(x, g1, sh1, sc1, w_in_bf, gq, gk, cos, sin, latent=True)
    lc = ctx.shape[1]
    k_c, v_c = _inproj(ctx, g1, csh1, csc1, w_in_bf, gq, gk, cos[:lc], sin[:lc], latent=False)

    attn = _attention(q, jnp.concatenate([k_l, k_c], axis=1), jnp.concatenate([v_l, v_c], axis=1))

    dft, chan = _dft_tables(l, f.shape[2] // F_GROUPS)
    four = _fourier(f, _fmat(chan, w_fourier[layer]), dft, b_fourier[layer])

    xl, h2, qp = _outproj(attn, four, x, w_out[layer].astype(BF16), gt1,
                          g_norm2[layer].reshape(1, d), sh2, sc2, w_query[layer].astype(BF16))

    g, idx_t = _retrieve(qp, sub_keys[layer].astype(BF16), tb)
    n_exp = u_experts.shape[1]
    uv = jnp.concatenate([u_experts[layer].astype(BF16).reshape(n_exp, d // LANES, LANES),
                          v_experts[layer].astype(BF16).reshape(n_exp, d // LANES, LANES)], axis=1)
    out = _peer(idx_t, uv, g, h2.reshape(b * l, d), xl.reshape(b * l, d), gt2,
                g_final.reshape(1, d), l // tb)
    return out.reshape(b, l, d)
```

```python
import functools
import math

import numpy as np
import jax
import jax.numpy as jnp
from jax import lax
from jax.experimental import pallas as pl
from jax.experimental.pallas import tpu as pltpu

GRID_W = 64
HEAD_DIM = 128
N_Q_HEADS = 8
N_KV_HEADS = 2
GQA_GROUP = N_Q_HEADS // N_KV_HEADS
ATTN_WIDTH = N_Q_HEADS * HEAD_DIM
KV_WIDTH = N_KV_HEADS * HEAD_DIM
ROPE_THETA = 10000.0
ROPE_FREQS = HEAD_DIM // 4
F_GROUPS = 4
PEER_HEADS = 8
TOPK = 16
N_MOD = 6
EPS = 1e-6

LANES = 128
VMEM_LIMIT = 56 * 1024 * 1024

F32 = jnp.float32
BF16 = jnp.bfloat16
NT_DIMS = (((1,), (1,)), ((), ()))


def _params(*sem):
    return pltpu.CompilerParams(dimension_semantics=sem, vmem_limit_bytes=VMEM_LIMIT)


def _resident(shape, index_map):
    return pl.BlockSpec(shape, index_map, pipeline_mode=pl.Buffered(1))


def _rms(x, g):
    return x * lax.rsqrt(jnp.mean(x * x, axis=-1, keepdims=True) + EPS) * g


def _ada_kernel(c_ref, w_ref, b_ref, o_ref):
    c = c_ref[...]
    a = (c * jax.nn.sigmoid(c)).astype(BF16)
    o_ref[...] = jnp.dot(a, w_ref[...].astype(BF16), preferred_element_type=F32) + b_ref[...]


def _ada(cc, w, b):
    rows, d = cc.shape
    n = w.shape[1]
    tn = next(t for t in (1024, 512, 256, LANES) if n % t == 0)
    return pl.pallas_call(
        _ada_kernel,
        grid=(n // tn,),
        in_specs=[pl.BlockSpec((rows, d), lambda j: (0, 0)),
                  pl.BlockSpec((d, tn), lambda j: (0, j)),
                  pl.BlockSpec((1, tn), lambda j: (0, j))],
        out_specs=pl.BlockSpec((rows, tn), lambda j: (0, j)),
        out_shape=jax.ShapeDtypeStruct((rows, n), F32),
        compiler_params=_params("parallel"),
        name="ada",
    )(cc, w, b.reshape(1, n))


def _rope(t, cos, sin_signed):
    lane = lax.broadcasted_iota(jnp.int32, t.shape, 1)
    partner = jnp.where((lane % 64) < 32, pltpu.roll(t, 96, 1), pltpu.roll(t, 32, 1))
    return t * cos + partner * sin_signed


def _inproj_kernel(x_ref, g1_ref, sh_ref, sc_ref, w_ref, gq_ref, gk_ref, cos_ref, sin_ref,
                   *out_refs, latent):
    h = (_rms(x_ref[0], g1_ref[...]) * (1.0 + sc_ref[0]) + sh_ref[0]).astype(BF16)
    if latent:
        q_ref, k_ref, v_ref, f_ref = out_refs
        cos, sin = cos_ref[...], sin_ref[...]
        q = jnp.dot(h, w_ref[:, 0:ATTN_WIDTH], preferred_element_type=F32)
        scale = HEAD_DIM ** -0.5
        for j in range(N_Q_HEADS):
            sl = slice(j * HEAD_DIM, (j + 1) * HEAD_DIM)
            q_ref[0, :, sl] = (_rope(_rms(q[:, sl], gq_ref[...]), cos, sin) * scale).astype(BF16)
        k0 = ATTN_WIDTH
    else:
        k_ref, v_ref = out_refs
        k0 = 0
    k = jnp.dot(h, w_ref[:, k0:k0 + KV_WIDTH], preferred_element_type=F32)
    for j in range(N_KV_HEADS):
        sl = slice(j * HEAD_DIM, (j + 1) * HEAD_DIM)
        t = _rms(k[:, sl], gk_ref[...])
        if latent:
            t = _rope(t, cos, sin)
        k_ref[0, :, sl] = t.astype(BF16)
    v_ref[0] = jnp.dot(h, w_ref[:, k0 + KV_WIDTH:k0 + 2 * KV_WIDTH],
                       preferred_element_type=F32).astype(BF16)
    if latent:
        f_ref[0] = jnp.dot(h, w_ref[:, ATTN_WIDTH + 2 * KV_WIDTH:],
                           preferred_element_type=F32).astype(BF16)


def _inproj(x, g1, sh, sc, w_bf, gq, gk, cos, sin, latent):
    b, l, d = x.shape
    tm = min(l, 512)
    wcols = w_bf.shape[1]
    fw = wcols - ATTN_WIDTH - 2 * KV_WIDTH
    row = lambda bi, i: (bi, i, 0)
    per_b = pl.BlockSpec((1, 1, d), lambda bi, i: (bi, 0, 0))
    vec = lambda n: pl.BlockSpec((1, n), lambda bi, i: (0, 0))
    if latent:
        w_spec = _resident((d, wcols), lambda bi, i: (0, 0))
        widths = (ATTN_WIDTH, KV_WIDTH, KV_WIDTH, fw)
    else:
        assert ATTN_WIDTH % (2 * KV_WIDTH) == 0
        w_spec = _resident((d, 2 * KV_WIDTH), lambda bi, i: (0, ATTN_WIDTH // (2 * KV_WIDTH)))
        widths = (KV_WIDTH, KV_WIDTH)
    return pl.pallas_call(
        functools.partial(_inproj_kernel, latent=latent),
        grid=(b, l // tm),
        in_specs=[pl.BlockSpec((1, tm, d), row), vec(d), per_b, per_b, w_spec,
                  vec(HEAD_DIM), vec(HEAD_DIM),
                  pl.BlockSpec((tm, HEAD_DIM), lambda bi, i: (i, 0)),
                  pl.BlockSpec((tm, HEAD_DIM), lambda bi, i: (i, 0))],
        out_specs=[pl.BlockSpec((1, tm, n), row) for n in widths],
        out_shape=[jax.ShapeDtypeStruct((b, l, n), BF16) for n in widths],
        compiler_params=_params("parallel", "parallel"),
        name="inproj_latent" if latent else "inproj_ctx",
    )(x, g1, sh, sc, w_bf, gq, gk, cos, sin)


def _attn_kernel(q_ref, k_ref, v_ref, o_ref):
    k = k_ref[0]
    v = v_ref[0]
    for j in range(GQA_GROUP):
        sl = slice(j * HEAD_DIM, (j + 1) * HEAD_DIM)
        s = lax.dot_general(q_ref[0, :, sl], k, NT_DIMS, preferred_element_type=F32)
        p = jnp.exp(s - jnp.max(s, axis=-1, keepdims=True))
        denom = jnp.sum(p, axis=-1, keepdims=True)
        o = jnp.dot(p.astype(BF16), v, preferred_element_type=F32) / denom
        o_ref[0, :, sl] = o.astype(BF16)


def _attention(q, k_all, v_all):
    b, l, _ = q.shape
    lk = k_all.shape[1]
    tq = min(l, 256)
    gw = GQA_GROUP * HEAD_DIM
    return pl.pallas_call(
        _attn_kernel,
        grid=(b, N_KV_HEADS, l // tq),
        in_specs=[pl.BlockSpec((1, tq, gw), lambda bi, g, i: (bi, i, g)),
                  pl.BlockSpec((1, lk, HEAD_DIM), lambda bi, g, i: (bi, 0, g)),
                  pl.BlockSpec((1, lk, HEAD_DIM), lambda bi, g, i: (bi, 0, g))],
        out_specs=pl.BlockSpec((1, tq, gw), lambda bi, g, i: (bi, i, g)),
        out_shape=jax.ShapeDtypeStruct((b, l, ATTN_WIDTH), BF16),
        compiler_params=_params("parallel", "parallel", "parallel"),
        name="attn",
    )(q, k_all, v_all)


def _dft_tables(l, c):
    kl = (np.arange(l, dtype=np.int64)[:, None] * np.arange(l, dtype=np.int64)[None, :]) % l
    ang = 2.0 * np.pi * kl.astype(np.float64) / l
    pos = np.concatenate([np.cos(ang), -np.sin(ang)], axis=1)
    jc = (np.arange(c, dtype=np.int64)[:, None] * np.arange(c, dtype=np.int64)[None, :]) % c
    angc = 2.0 * np.pi * jc.astype(np.float64) / c
    chan = np.stack([np.cos(angc), np.sin(angc)]) / math.sqrt(l * c)

    def split(t):
        head = t.astype(np.float32)
        return jnp.asarray(head) + jnp.asarray((t - head).astype(np.float32))

    return split(pos).astype(BF16), split(chan)


def _fmat_kernel(cs_ref, w_ref, o_ref):
    c = w_ref.shape[-1]
    w = w_ref[0]
    o_ref[0, :, 0:c] = jnp.dot(cs_ref[0], w, preferred_element_type=F32,
                               precision=lax.Precision.HIGHEST).astype(BF16)
    o_ref[0, :, c:2 * c] = jnp.dot(cs_ref[1], w, preferred_element_type=F32,
                                   precision=lax.Precision.HIGHEST).astype(BF16)


def _fmat(chan, w_f):
    g, c, _ = w_f.shape
    return pl.pallas_call(
        _fmat_kernel,
        grid=(g,),
        in_specs=[pl.BlockSpec((2, c, c), lambda gi: (0, 0, 0)),
                  pl.BlockSpec((1, c, c), lambda gi: (gi, 0, 0))],
        out_specs=pl.BlockSpec((1, c, 2 * c), lambda gi: (gi, 0, 0)),
        out_shape=jax.ShapeDtypeStruct((g, c, 2 * c), BF16),
        compiler_params=_params("parallel"),
        name="fmat",
    )(chan, w_f)


def _fourier_kernel(x_ref, m_ref, dft_ref, b_ref, o_ref, z_ref, *, tr):
    l, c = x_ref.shape[1], x_ref.shape[2]
    z = jnp.dot(x_ref[0], m_ref[0], preferred_element_type=F32)
    z_ref[0:l, :] = z[:, 0:c].astype(BF16)
    z_ref[l:2 * l, :] = z[:, c:2 * c].astype(BF16)
    for r in range(l // tr):
        rows = slice(r * tr, (r + 1) * tr)
        y = jnp.dot(dft_ref[rows, :], z_ref[...], preferred_element_type=F32) + b_ref[0]
        o_ref[0, rows, :] = y.astype(BF16)


def _fourier(f, fm, dft, b_f):
    b, l, fw = f.shape
    c = fw // F_GROUPS
    tr = min(l, 512)
    return pl.pallas_call(
        functools.partial(_fourier_kernel, tr=tr),
        grid=(b, F_GROUPS),
        in_specs=[pl.BlockSpec((1, l, c), lambda bi, g: (bi, 0, g)),
                  pl.BlockSpec((1, c, 2 * c), lambda bi, g: (g, 0, 0)),
                  _resident((l, 2 * l), lambda bi, g: (0, 0)),
                  pl.BlockSpec((1, 1, c), lambda bi, g: (g, 0, 0))],
        out_specs=pl.BlockSpec((1, l, c), lambda bi, g: (bi, 0, g)),
        out_shape=jax.ShapeDtypeStruct((b, l, fw), BF16),
        scratch_shapes=[pltpu.VMEM((2 * l, c), BF16)],
        compiler_params=_params("parallel", "parallel"),
        name="fourier",
    )(f, fm, dft, b_f.reshape(F_GROUPS, 1, c))


def _outproj_kernel(a_ref, f_ref, x_ref, wo_ref, gt1_ref, g2_ref, sh2_ref, sc2_ref, wq_ref,
                    xl_ref, h2_ref, qp_ref):
    aw = a_ref.shape[2]
    y = jnp.dot(a_ref[0], wo_ref[0:aw, :], preferred_element_type=F32)
    y = y + jnp.dot(f_ref[0], wo_ref[aw:, :], preferred_element_type=F32)
    xl = x_ref[0] + gt1_ref[0] * y
    xl_ref[0] = xl
    h2 = _rms(xl, g2_ref[...]) * (1.0 + sc2_ref[0]) + sh2_ref[0]
    h2_ref[0] = h2
    qp = jnp.dot(h2.astype(BF16), wq_ref[...], preferred_element_type=F32)
    for j in range(qp_ref.shape[0]):
        qp_ref[j] = qp[:, j * LANES:(j + 1) * LANES].astype(BF16)


def _outproj(attn, four, x, wo_bf, gt1, g2, sh2, sc2, wq_bf):
    b, l, d = x.shape
    tm = min(l, 256)
    nt = l // tm
    aw, fw = attn.shape[2], four.shape[2]
    qw = wq_bf.shape[1]
    row = lambda bi, i: (bi, i, 0)
    per_b = pl.BlockSpec((1, 1, d), lambda bi, i: (bi, 0, 0))
    return pl.pallas_call(
        _outproj_kernel,
        grid=(b, nt),
        in_specs=[pl.BlockSpec((1, tm, aw), row), pl.BlockSpec((1, tm, fw), row),
                  pl.BlockSpec((1, tm, d), row),
                  _resident((aw + fw, d), lambda bi, i: (0, 0)),
                  per_b, pl.BlockSpec((1, d), lambda bi, i: (0, 0)), per_b, per_b,
                  _resident((d, qw), lambda bi, i: (0, 0))],
        out_specs=[pl.BlockSpec((1, tm, d), row), pl.BlockSpec((1, tm, d), row),
                   pl.BlockSpec((qw // LANES, tm, LANES), lambda bi, i: (0, bi * nt + i, 0))],
        out_shape=[jax.ShapeDtypeStruct((b, l, d), F32), jax.ShapeDtypeStruct((b, l, d), F32),
                   jax.ShapeDtypeStruct((qw // LANES, b * l, LANES), BF16)],
        compiler_params=_params("parallel", "parallel"),
        name="outproj",
    )(attn, four, x, wo_bf, gt1, g2, sh2, sc2, wq_bf)


def _topk_rows(s, k, payload=None):
    rows = s.shape[0]
    iota = lax.broadcasted_iota(jnp.int32, s.shape, 0).astype(F32)
    vals, sel = [], []
    for _ in range(k):
        m = jnp.max(s, axis=0, keepdims=True)
        i = jnp.min(jnp.where(s == m, iota, float(rows)), axis=0, keepdims=True)
        hit = iota == i
        vals.append(m)
        sel.append(i if payload is None else
                   jnp.sum(jnp.where(hit, payload, 0.0), axis=0, keepdims=True))
        s = jnp.where(hit, -jnp.inf, s)
    return jnp.concatenate(vals, axis=0), jnp.concatenate(sel, axis=0)


def _pair_candidates(s1, i1, s2, i2, n_keys):
    assert TOPK == 16
    sub = lax.broadcasted_iota(jnp.int32, (8, s1.shape[1]), 0)
    ids = [i1[0:1] * n_keys + i2, i1[1:2] * n_keys + i2[0:8]]
    vals = [s1[0:1] + s2, s1[1:2] + s2[0:8]]
    for a in range(2, 8):
        keep = sub < TOPK // (a + 1)
        vals.append(jnp.where(keep, s1[a:a + 1] + s2[0:8], -jnp.inf))
        ids.append(i1[a:a + 1] * n_keys + i2[0:8])
    vals.append(s1[8:TOPK] + s2[0:1])
    ids.append(i1[8:TOPK] * n_keys + i2[0:1])
    return jnp.concatenate(vals, axis=0), jnp.concatenate(ids, axis=0)


def _retrieve_kernel(q_ref, keys_ref, g_ref, idx_ref, gs_ref, is_ref):
    n_keys = keys_ref.shape[2]

    tokens = q_ref.shape[1]

    def head_pair(hp, carry):
        cands, cidxs = [], []
        for hh in range(2):
            h = 2 * hp + hh
            s = jnp.concatenate(
                [lax.dot_general(keys_ref[h, p], q_ref[2 * h + p], NT_DIMS,
                                 preferred_element_type=F32) for p in range(2)],
                axis=1)
            sv, si = _topk_rows(s, TOPK)
            cand, cidx = _pair_candidates(sv[:, :tokens], si[:, :tokens],
                                          sv[:, tokens:], si[:, tokens:], float(n_keys))
            cands.append(cand)
            cidxs.append(cidx)
        sf, eidx = _topk_rows(jnp.concatenate(cands, axis=1), TOPK,
                              payload=jnp.concatenate(cidxs, axis=1))
        e = jnp.exp(sf - sf[0:1])
        g = e / jnp.sum(e, axis=0, keepdims=True)
        for hh in range(2):
            rows = pl.ds(pl.multiple_of((2 * hp + hh) * TOPK, TOPK), TOPK)
            lanes = slice(hh * tokens, (hh + 1) * tokens)
            gs_ref[rows, :] = g[:, lanes]
            is_ref[rows, :] = eidx[:, lanes].astype(jnp.int32)
        return carry

    assert PEER_HEADS % 2 == 0
    lax.fori_loop(0, PEER_HEADS // 2, head_pair, 0)
    g_ref[...] = gs_ref[...].T
    for blk in range(idx_ref.shape[0]):
        idx_ref[blk] = is_ref[:, blk * LANES:(blk + 1) * LANES]


RETRIEVE_BLOCKS = 2


def _retrieve(qp, keys_bf, tb):
    nq, n, _ = qp.shape
    npick = PEER_HEADS * TOPK
    assert tb == LANES and npick == LANES
    nblk = n // tb
    per_step = RETRIEVE_BLOCKS if nblk % RETRIEVE_BLOCKS == 0 else 1
    ts = per_step * tb
    return pl.pallas_call(
        _retrieve_kernel,
        grid=(nblk // per_step,),
        in_specs=[pl.BlockSpec((nq, ts, LANES), lambda i: (0, i, 0)),
                  pl.BlockSpec(keys_bf.shape, lambda i: (0, 0, 0, 0))],
        out_specs=[pl.BlockSpec((ts, npick), lambda i: (i, 0)),
                   pl.BlockSpec((per_step, npick, tb), lambda i: (i, 0, 0))],
        out_shape=[jax.ShapeDtypeStruct((n, npick), F32),
                   jax.ShapeDtypeStruct((nblk, npick, tb), jnp.int32)],
        scratch_shapes=[pltpu.VMEM((npick, ts), F32), pltpu.VMEM((npick, ts), jnp.int32)],
        compiler_params=_params("parallel"),
        name="retrieve",
    )(qp, keys_bf)


N_SLOTS = 8
BITREV3 = (0, 4, 2, 6, 1, 5, 3, 7)
DMA_THREADS = 2


def _peer_kernel(idx_hbm, uv_hbm, g_ref, h2_ref, h2n_ref, xl_ref, gt2_ref, gf_ref, o_ref,
                 idx_smem, idx_sem, sem, peer_ref, stage_ref, wb0, wb1, part0, part1, *bufs):
    tb, d = h2_ref.shape
    npick = g_ref.shape[1]
    nc = d // LANES
    blk_words = npick * tb
    i = pl.program_id(0)
    nblk = pl.num_programs(0)
    cur = i % 2
    nxt = 1 - cur
    has_next = i + 1 < nblk
    ahead = N_SLOTS - 1

    def idx_copy(blk, half):
        return pltpu.make_async_copy(
            idx_hbm.at[pl.ds(pl.multiple_of(blk * blk_words, blk_words), blk_words)],
            idx_smem.at[pl.ds(pl.multiple_of(half * blk_words, blk_words), blk_words)],
            idx_sem.at[half])

    def slab_copy(word, k, slot):
        return pltpu.make_async_copy(uv_hbm.at[idx_smem[word]], bufs[slot].at[k], sem.at[slot])

    def wait(slot):
        pltpu.make_async_copy(uv_hbm.at[pl.ds(0, npick)], bufs[slot], sem.at[slot]).wait()

    sub = lax.broadcasted_iota(jnp.int32, (8, LANES), 0)

    def fold(a, b, dist):
        m = (sub & dist) == 0
        if 2 * dist == 8:
            return jnp.where(m, a, b) + pltpu.roll(jnp.where(m, b, a), dist, 0)
        return (jnp.where(m, a, b)
                + jnp.where(m, pltpu.roll(a, 8 - dist, 0), pltpu.roll(b, dist, 0)))

    def u_side(xrow, slot, part_ref, start_copy=None):
        x = jnp.concatenate([xrow[:, c * LANES:(c + 1) * LANES] for c in range(nc)], axis=0)

        def partial(k):
            p = bufs[slot][k, 0:nc, :].astype(F32) * x
            acc = p[0:8]
            for r in range(1, nc // 8):
                acc = acc + p[8 * r:8 * r + 8]
            return acc

        for grp in range(npick // 8):
            if start_copy is not None:
                for j in range(4):
                    start_copy(grp * 4 + j)
            ps = [partial(grp * 8 + BITREV3[j]) for j in range(8)]
            q = [fold(ps[2 * j], ps[2 * j + 1], 4) for j in range(4)]
            part_ref[grp * 8:grp * 8 + 8, :] = fold(fold(q[0], q[1], 2), fold(q[2], q[3], 2), 1)

    def weights(grow, part_ref, wb_ref):
        a = jnp.sum(part_ref[...].T, axis=0, keepdims=True)
        gelu = 0.5 * a * (1.0 + lax.erf(a * math.sqrt(0.5)))
        wb_ref[...] = jnp.broadcast_to(grow * gelu, (npick, npick)).T

    def v_side(u, slot, wb_ref, start_copy):
        accs = [None] * 4
        for k in range(npick):
            if k % 2 == 0:
                start_copy(npick // 2 + k // 2)
            term = wb_ref[k:k + 1, :] * bufs[slot][k, nc:2 * nc, :].astype(F32)
            accs[k % 4] = term if accs[k % 4] is None else accs[k % 4] + term
        out = (accs[0] + accs[1]) + (accs[2] + accs[3])

        for c in range(nc):
            stage_ref[u:u + 1, c * LANES:(c + 1) * LANES] = out[c:c + 1, :]

    parts = (part0, part1)
    wbs = (wb0, wb1)

    @pl.when(i == 0)
    def _():
        first = idx_copy(0, 0)
        first.start()
        first.wait()
        for t in range(ahead):
            for k in range(npick):
                slab_copy(k * tb + t, k, t).start()
        wait(0)
        u_side(h2_ref[0:1, :], 0, parts[0])

    @pl.when(has_next)
    def _():
        idx_copy(i + 1, nxt).start()

    next_base = jnp.where(has_next, nxt, cur) * blk_words - tb

    last = tb // N_SLOTS - 1

    def row_of(ref, next_ref, r, u, reach):
        row = ref[pl.ds(jnp.minimum(r, tb - 1), 1), :]
        over = u + reach - N_SLOTS
        if over >= 0:
            row = jnp.where(r < tb, row, next_ref[over:over + 1, :])
        return row

    def group_of_tokens(j, carry):
        @pl.when(jnp.logical_and(j == last, has_next))
        def _():
            idx_copy(i + 1, nxt).wait()

        for u in range(N_SLOTS):
            t = j * N_SLOTS + u
            ta = t + ahead
            word0 = jnp.where(ta < tb, cur * blk_words, next_base) + ta
            fill = (u + ahead) % N_SLOTS

            def start_copy(k, word0=word0, fill=fill):
                slab_copy(word0 + k * tb, k, fill).start(priority=k % DMA_THREADS)

            wait((u + 1) % N_SLOTS)
            weights(g_ref[pl.ds(t, 1), :], parts[u % 2], wbs[u % 2])
            u_side(row_of(h2_ref, h2n_ref, t + 1, u, 1), (u + 1) % N_SLOTS, parts[(u + 1) % 2],
                   start_copy)
            v_side(u, u, wbs[u % 2], start_copy)
        peer_ref[pl.ds(pl.multiple_of(j * N_SLOTS, N_SLOTS), N_SLOTS), :] = stage_ref[...]
        return carry

    lax.fori_loop(0, last + 1, group_of_tokens, 0)

    @pl.when(jnp.logical_not(has_next))
    def _():
        for s in range(1, ahead):
            wait(s)

    y = xl_ref[...] + gt2_ref[0] * peer_ref[...]
    o_ref[...] = _rms(y, gf_ref[...])


def _peer(idx_t, uv, g, h2, xl, gt2, gf, blocks_per_batch):
    nblk, npick, tb = idx_t.shape
    n, d = h2.shape
    nc = d // LANES
    assert tb % N_SLOTS == 0 and N_SLOTS % 8 == 0 and npick % 8 == 0 and nc % 16 == 0
    assert uv.shape[1:] == (2 * nc, LANES) and uv.dtype == BF16
    row = lambda i: (i, 0)
    next_rows = lambda i: (jnp.minimum(i + 1, nblk - 1) * (tb // 8), 0)
    return pl.pallas_call(
        _peer_kernel,
        grid=(nblk,),
        in_specs=[pl.BlockSpec(memory_space=pl.ANY), pl.BlockSpec(memory_space=pl.ANY),
                  pl.BlockSpec((tb, npick), row),
                  pl.BlockSpec((tb, d), row), pl.BlockSpec((8, d), next_rows),
                  pl.BlockSpec((tb, d), row),
                  pl.BlockSpec((1, 1, d), lambda i: (i // blocks_per_batch, 0, 0)),
                  pl.BlockSpec((1, d), lambda i: (0, 0))],
        out_specs=pl.BlockSpec((tb, d), row),
        out_shape=jax.ShapeDtypeStruct((n, d), F32),
        scratch_shapes=[pltpu.SMEM((2 * npick * tb,), jnp.int32), pltpu.SemaphoreType.DMA((2,)),
                        pltpu.SemaphoreType.DMA((N_SLOTS,)), pltpu.VMEM((tb, d), F32),
                        pltpu.VMEM((N_SLOTS, d), F32),
                        pltpu.VMEM((npick, npick), F32), pltpu.VMEM((npick, npick), F32),
                        pltpu.VMEM((npick, LANES), F32), pltpu.VMEM((npick, LANES), F32)]
        + [pltpu.VMEM((npick, 2 * nc, LANES), BF16) for _ in range(N_SLOTS)],
        compiler_params=_params("arbitrary"),
        name="peer",
    )(idx_t.reshape(-1), uv, g, h2, h2, xl, gt2, gf)


def _rope_tables(length):
    rows = length // GRID_W
    row = jnp.broadcast_to(jnp.arange(rows)[:, None], (rows, GRID_W)).reshape(-1)
    col = jnp.broadcast_to(jnp.arange(GRID_W)[None, :], (rows, GRID_W)).reshape(-1)
    inv_freq = ROPE_THETA ** (-jnp.arange(ROPE_FREQS, dtype=F32) / ROPE_FREQS)
    ar = row.astype(F32)[:, None] * inv_freq
    ac = col.astype(F32)[:, None] * inv_freq
    cos = jnp.concatenate([jnp.cos(ar), jnp.cos(ar), jnp.cos(ac), jnp.cos(ac)], axis=-1)
    sin = jnp.concatenate([-jnp.sin(ar), jnp.sin(ar), -jnp.sin(ac), jnp.sin(ac)], axis=-1)
    return cos, sin


def kernel(x, c, ctx, c_ctx, w_ada, b_ada, g_norm1, w_in, g_q, g_k, w_fourier, b_fourier,
           w_out, g_norm2, w_query, sub_keys, u_experts, v_experts, g_final):
    b, l, d = x.shape
    assert w_ada.shape[0] == 1, "single-layer configuration only"
    layer = 0
    tb = LANES

    rows = -(-(b + 1) // 8) * 8
    cc = jnp.concatenate([c, c_ctx[None, :], jnp.zeros((rows - b - 1, d), F32)], axis=0)
    mod = _ada(cc, w_ada[layer], b_ada[layer])
    sh1, sc1, gt1, sh2, sc2, gt2 = [m.reshape(b, 1, d) for m in jnp.split(mod[:b], N_MOD, axis=-1)]
    csh1, csc1 = [jnp.broadcast_to(m.reshape(1, 1, d), (b, 1, d))
                  for m in jnp.split(mod[b], N_MOD, axis=-1)[:2]]

    cos, sin = _rope_tables(l)
    g1 = g_norm1[layer].reshape(1, d)
    gq = g_q[layer].reshape(1, HEAD_DIM)
    gk = g_k[layer].reshape(1, HEAD_DIM)
    w_in_bf = w_in[layer].astype(BF16)
    q, k_l, v_l, f = _inproj(x, g1, sh1, sc1, w_in_bf, gq, gk, cos, sin, latent=True)
    lc = ctx.shape[1]
    k_c, v_c = _inproj(ctx, g1, csh1, csc1, w_in_bf, gq, gk, cos[:lc], sin[:lc], latent=False)

    attn = _attention(q, jnp.concatenate([k_l, k_c], axis=1), jnp.concatenate([v_l, v_c], axis=1))

    dft, chan = _dft_tables(l, f.shape[2] // F_GROUPS)
    four = _fourier(f, _fmat(chan, w_fourier[layer]), dft, b_fourier[layer])

    xl, h2, qp = _outproj(attn, four, x, w_out[layer].astype(BF16), gt1,
                          g_norm2[layer].reshape(1, d), sh2, sc2, w_query[layer].astype(BF16))

    g, idx_t = _retrieve(qp, sub_keys[layer].astype(BF16), tb)
    n_exp = u_experts.shape[1]
    uv = jnp.concatenate([u_experts[layer].astype(BF16).reshape(n_exp, d // LANES, LANES),
                          v_experts[layer].astype(BF16).reshape(n_exp, d // LANES, LANES)], axis=1)
    out = _peer(idx_t, uv, g, h2.reshape(b * l, d), xl.reshape(b * l, d), gt2,
                g_final.reshape(1, d), l // tb)
    return out.reshape(b, l, d)
```

```python
import functools
import math

import numpy as np
import jax
import jax.numpy as jnp
from jax import lax
from jax.experimental import pallas as pl
from jax.experimental.pallas import tpu as pltpu

GRID_W = 64
HEAD_DIM = 128
N_Q_HEADS = 8
N_KV_HEADS = 2
GQA_GROUP = N_Q_HEADS // N_KV_HEADS
ATTN_WIDTH = N_Q_HEADS * HEAD_DIM
KV_WIDTH = N_KV_HEADS * HEAD_DIM
ROPE_THETA = 10000.0
ROPE_FREQS = HEAD_DIM // 4
F_GROUPS = 4
PEER_HEADS = 8
TOPK = 16
N_MOD = 6
EPS = 1e-6

LANES = 128
VMEM_LIMIT = 56 * 1024 * 1024

F32 = jnp.float32
BF16 = jnp.bfloat16
NT_DIMS = (((1,), (1,)), ((), ()))


def _params(*sem):
    return pltpu.CompilerParams(dimension_semantics=sem, vmem_limit_bytes=VMEM_LIMIT)


def _resident(shape, index_map):
    return pl.BlockSpec(shape, index_map, pipeline_mode=pl.Buffered(1))


def _rms(x, g):
    return x * lax.rsqrt(jnp.mean(x * x, axis=-1, keepdims=True) + EPS) * g


def _ada_kernel(c_ref, w_ref, b_ref, o_ref):
    c = c_ref[...]
    a = (c * jax.nn.sigmoid(c)).astype(BF16)
    o_ref[...] = jnp.dot(a, w_ref[...].astype(BF16), preferred_element_type=F32) + b_ref[...]


def _ada(cc, w, b):
    rows, d = cc.shape
    n = w.shape[1]
    tn = next(t for t in (1024, 512, 256, LANES) if n % t == 0)
    return pl.pallas_call(
        _ada_kernel,
        grid=(n // tn,),
        in_specs=[pl.BlockSpec((rows, d), lambda j: (0, 0)),
                  pl.BlockSpec((d, tn), lambda j: (0, j)),
                  pl.BlockSpec((1, tn), lambda j: (0, j))],
        out_specs=pl.BlockSpec((rows, tn), lambda j: (0, j)),
        out_shape=jax.ShapeDtypeStruct((rows, n), F32),
        compiler_params=_params("parallel"),
        name="ada",
    )(cc, w, b.reshape(1, n))


def _rope(t, cos, sin_signed):
    lane = lax.broadcasted_iota(jnp.int32, t.shape, 1)
    partner = jnp.where((lane % 64) < 32, pltpu.roll(t, 96, 1), pltpu.roll(t, 32, 1))
    return t * cos + partner * sin_signed


def _inproj_kernel(x_ref, g1_ref, sh_ref, sc_ref, w_ref, gq_ref, gk_ref, cos_ref, sin_ref,
                   *out_refs, latent):
    h = (_rms(x_ref[0], g1_ref[...]) * (1.0 + sc_ref[0]) + sh_ref[0]).astype(BF16)
    if latent:
        q_ref, k_ref, v_ref, f_ref = out_refs
        cos, sin = cos_ref[...], sin_ref[...]
        q = jnp.dot(h, w_ref[:, 0:ATTN_WIDTH], preferred_element_type=F32)
        scale = HEAD_DIM ** -0.5
        for j in range(N_Q_HEADS):
            sl = slice(j * HEAD_DIM, (j + 1) * HEAD_DIM)
            q_ref[0, :, sl] = (_rope(_rms(q[:, sl], gq_ref[...]), cos, sin) * scale).astype(BF16)
        k0 = ATTN_WIDTH
    else:
        k_ref, v_ref = out_refs
        k0 = 0
    k = jnp.dot(h, w_ref[:, k0:k0 + KV_WIDTH], preferred_element_type=F32)
    for j in range(N_KV_HEADS):
        sl = slice(j * HEAD_DIM, (j + 1) * HEAD_DIM)
        t = _rms(k[:, sl], gk_ref[...])
        if latent:
            t = _rope(t, cos, sin)
        k_ref[0, :, sl] = t.astype(BF16)
    v_ref[0] = jnp.dot(h, w_ref[:, k0 + KV_WIDTH:k0 + 2 * KV_WIDTH],
                       preferred_element_type=F32).astype(BF16)
    if latent:
        f_ref[0] = jnp.dot(h, w_ref[:, ATTN_WIDTH + 2 * KV_WIDTH:],
                           preferred_element_type=F32).astype(BF16)


def _inproj(x, g1, sh, sc, w_bf, gq, gk, cos, sin, latent):
    b, l, d = x.shape
    tm = min(l, 512)
    wcols = w_bf.shape[1]
    fw = wcols - ATTN_WIDTH - 2 * KV_WIDTH
    row = lambda bi, i: (bi, i, 0)
    per_b = pl.BlockSpec((1, 1, d), lambda bi, i: (bi, 0, 0))
    vec = lambda n: pl.BlockSpec((1, n), lambda bi, i: (0, 0))
    if latent:
        w_spec = _resident((d, wcols), lambda bi, i: (0, 0))
        widths = (ATTN_WIDTH, KV_WIDTH, KV_WIDTH, fw)
    else:
        assert ATTN_WIDTH % (2 * KV_WIDTH) == 0
        w_spec = _resident((d, 2 * KV_WIDTH), lambda bi, i: (0, ATTN_WIDTH // (2 * KV_WIDTH)))
        widths = (KV_WIDTH, KV_WIDTH)
    return pl.pallas_call(
        functools.partial(_inproj_kernel, latent=latent),
        grid=(b, l // tm),
        in_specs=[pl.BlockSpec((1, tm, d), row), vec(d), per_b, per_b, w_spec,
                  vec(HEAD_DIM), vec(HEAD_DIM),
                  pl.BlockSpec((tm, HEAD_DIM), lambda bi, i: (i, 0)),
                  pl.BlockSpec((tm, HEAD_DIM), lambda bi, i: (i, 0))],
        out_specs=[pl.BlockSpec((1, tm, n), row) for n in widths],
        out_shape=[jax.ShapeDtypeStruct((b, l, n), BF16) for n in widths],
        compiler_params=_params("parallel", "parallel"),
        name="inproj_latent" if latent else "inproj_ctx",
    )(x, g1, sh, sc, w_bf, gq, gk, cos, sin)


def _attn_kernel(q_ref, k_ref, v_ref, o_ref):
    k = k_ref[0]
    v = v_ref[0]
    for j in range(GQA_GROUP):
        sl = slice(j * HEAD_DIM, (j + 1) * HEAD_DIM)
        s = lax.dot_general(q_ref[0, :, sl], k, NT_DIMS, preferred_element_type=F32)
        p = jnp.exp(s - jnp.max(s, axis=-1, keepdims=True))
        denom = jnp.sum(p, axis=-1, keepdims=True)
        o = jnp.dot(p.astype(BF16), v, preferred_element_type=F32) / denom
        o_ref[0, :, sl] = o.astype(BF16)


def _attention(q, k_all, v_all):
    b, l, _ = q.shape
    lk = k_all.shape[1]
    tq = min(l, 256)
    gw = GQA_GROUP * HEAD_DIM
    return pl.pallas_call(
        _attn_kernel,
        grid=(b, N_KV_HEADS, l // tq),
        in_specs=[pl.BlockSpec((1, tq, gw), lambda bi, g, i: (bi, i, g)),
                  pl.BlockSpec((1, lk, HEAD_DIM), lambda bi, g, i: (bi, 0, g)),
                  pl.BlockSpec((1, lk, HEAD_DIM), lambda bi, g, i: (bi, 0, g))],
        out_specs=pl.BlockSpec((1, tq, gw), lambda bi, g, i: (bi, i, g)),
        out_shape=jax.ShapeDtypeStruct((b, l, ATTN_WIDTH), BF16),
        compiler_params=_params("parallel", "parallel", "parallel"),
        name="attn",
    )(q, k_all, v_all)


def _dft_tables(l, c):
    kl = (np.arange(l, dtype=np.int64)[:, None] * np.arange(l, dtype=np.int64)[None, :]) % l
    ang = 2.0 * np.pi * kl.astype(np.float64) / l
    pos = np.concatenate([np.cos(ang), -np.sin(ang)], axis=1)
    jc = (np.arange(c, dtype=np.int64)[:, None] * np.arange(c, dtype=np.int64)[None, :]) % c
    angc = 2.0 * np.pi * jc.astype(np.float64) / c
    chan = np.stack([np.cos(angc), np.sin(angc)]) / math.sqrt(l * c)

    def split(t):
        head = t.astype(np.float32)
        return jnp.asarray(head) + jnp.asarray((t - head).astype(np.float32))

    return split(pos).astype(BF16), split(chan)


def _fmat_kernel(cs_ref, w_ref, o_ref):
    c = w_ref.shape[-1]
    w = w_ref[0]
    o_ref[0, :, 0:c] = jnp.dot(cs_ref[0], w, preferred_element_type=F32,
                               precision=lax.Precision.HIGHEST).astype(BF16)
    o_ref[0, :, c:2 * c] = jnp.dot(cs_ref[1], w, preferred_element_type=F32,
                                   precision=lax.Precision.HIGHEST).astype(BF16)


def _fmat(chan, w_f):
    g, c, _ = w_f.shape
    return pl.pallas_call(
        _fmat_kernel,
        grid=(g,),
        in_specs=[pl.BlockSpec((2, c, c), lambda gi: (0, 0, 0)),
                  pl.BlockSpec((1, c, c), lambda gi: (gi, 0, 0))],
        out_specs=pl.BlockSpec((1, c, 2 * c), lambda gi: (gi, 0, 0)),
        out_shape=jax.ShapeDtypeStruct((g, c, 2 * c), BF16),
        compiler_params=_params("parallel"),
        name="fmat",
    )(chan, w_f)


def _fourier_kernel(x_ref, m_ref, dft_ref, b_ref, o_ref, z_ref, *, tr):
    l, c = x_ref.shape[1], x_ref.shape[2]
    z = jnp.dot(x_ref[0], m_ref[0], preferred_element_type=F32)
    z_ref[0:l, :] = z[:, 0:c].astype(BF16)
    z_ref[l:2 * l, :] = z[:, c:2 * c].astype(BF16)
    for r in range(l // tr):
        rows = slice(r * tr, (r + 1) * tr)
        y = jnp.dot(dft_ref[rows, :], z_ref[...], preferred_element_type=F32) + b_ref[0]
        o_ref[0, rows, :] = y.astype(BF16)


def _fourier(f, fm, dft, b_f):
    b, l, fw = f.shape
    c = fw // F_GROUPS
    tr = min(l, 512)
    return pl.pallas_call(
        functools.partial(_fourier_kernel, tr=tr),
        grid=(b, F_GROUPS),
        in_specs=[pl.BlockSpec((1, l, c), lambda bi, g: (bi, 0, g)),
                  pl.BlockSpec((1, c, 2 * c), lambda bi, g: (g, 0, 0)),
                  _resident((l, 2 * l), lambda bi, g: (0, 0)),
                  pl.BlockSpec((1, 1, c), lambda bi, g: (g, 0, 0))],
        out_specs=pl.BlockSpec((1, l, c), lambda bi, g: (bi, 0, g)),
        out_shape=jax.ShapeDtypeStruct((b, l, fw), BF16),
        scratch_shapes=[pltpu.VMEM((2 * l, c), BF16)],
        compiler_params=_params("parallel", "parallel"),
        name="fourier",
    )(f, fm, dft, b_f.reshape(F_GROUPS, 1, c))


def _outproj_kernel(a_ref, f_ref, x_ref, wo_ref, gt1_ref, g2_ref, sh2_ref, sc2_ref, wq_ref,
                    xl_ref, h2_ref, qp_ref):
    aw = a_ref.shape[2]
    y = jnp.dot(a_ref[0], wo_ref[0:aw, :], preferred_element_type=F32)
    y = y + jnp.dot(f_ref[0], wo_ref[aw:, :], preferred_element_type=F32)
    xl = x_ref[0] + gt1_ref[0] * y
    xl_ref[0] = xl
    h2 = _rms(xl, g2_ref[...]) * (1.0 + sc2_ref[0]) + sh2_ref[0]
    h2_ref[0] = h2
    qp = jnp.dot(h2.astype(BF16), wq_ref[...], preferred_element_type=F32)
    for j in range(qp_ref.shape[0]):
        qp_ref[j] = qp[:, j * LANES:(j + 1) * LANES].astype(BF16)


def _outproj(attn, four, x, wo_bf, gt1, g2, sh2, sc2, wq_bf):
    b, l, d = x.shape
    tm = min(l, 256)
    nt = l // tm
    aw, fw = attn.shape[2], four.shape[2]
    qw = wq_bf.shape[1]
    row = lambda bi, i: (bi, i, 0)
    per_b = pl.BlockSpec((1, 1, d), lambda bi, i: (bi, 0, 0))
    return pl.pallas_call(
        _outproj_kernel,
        grid=(b, nt),
        in_specs=[pl.BlockSpec((1, tm, aw), row), pl.BlockSpec((1, tm, fw), row),
                  pl.BlockSpec((1, tm, d), row),
                  _resident((aw + fw, d), lambda bi, i: (0, 0)),
                  per_b, pl.BlockSpec((1, d), lambda bi, i: (0, 0)), per_b, per_b,
                  _resident((d, qw), lambda bi, i: (0, 0))],
        out_specs=[pl.BlockSpec((1, tm, d), row), pl.BlockSpec((1, tm, d), row),
                   pl.BlockSpec((qw // LANES, tm, LANES), lambda bi, i: (0, bi * nt + i, 0))],
        out_shape=[jax.ShapeDtypeStruct((b, l, d), F32), jax.ShapeDtypeStruct((b, l, d), F32),
                   jax.ShapeDtypeStruct((qw // LANES, b * l, LANES), BF16)],
        compiler_params=_params("parallel", "parallel"),
        name="outproj",
    )(attn, four, x, wo_bf, gt1, g2, sh2, sc2, wq_bf)


def _topk_rows(s, k, payload=None):
    rows, lanes = s.shape
    assert rows % 8 == 0
    sub = lax.broadcasted_iota(jnp.int32, (8, lanes), 0).astype(F32)
    vals = [s[8 * g:8 * g + 8] for g in range(rows // 8)]
    ids = [sub + float(8 * g) for g in range(rows // 8)]
    pay = None if payload is None else [payload[8 * g:8 * g + 8] for g in range(rows // 8)]
    out_v, out_s = [], []
    for _ in range(k):
        v, i = vals, ids
        while len(v) > 1:
            nv, ni = [], []
            for a in range(0, len(v) - 1, 2):
                nv.append(jnp.maximum(v[a], v[a + 1]))
                ni.append(jnp.where(v[a] >= v[a + 1], i[a], i[a + 1]))
            if len(v) % 2:
                nv.append(v[-1])
                ni.append(i[-1])
            v, i = nv, ni
        m = jnp.max(v[0], axis=0, keepdims=True)
        win = jnp.min(jnp.where(v[0] == m, i[0], float(rows)), axis=0, keepdims=True)
        hits = [g == win for g in ids]
        out_v.append(m)
        if pay is None:
            out_s.append(win)
        else:
            picked = [jnp.where(h, p, 0.0) for h, p in zip(hits, pay)]
            out_s.append(jnp.sum(functools.reduce(jnp.add, picked), axis=0, keepdims=True))
        vals = [jnp.where(h, -jnp.inf, g) for h, g in zip(hits, vals)]
    return jnp.concatenate(out_v, axis=0), jnp.concatenate(out_s, axis=0)


def _pair_candidates(s1, i1, s2, i2, n_keys):
    assert TOPK == 16
    sub = lax.broadcasted_iota(jnp.int32, (8, s1.shape[1]), 0)
    ids = [i1[0:1] * n_keys + i2, i1[1:2] * n_keys + i2[0:8]]
    vals = [s1[0:1] + s2, s1[1:2] + s2[0:8]]
    for a in range(2, 8):
        keep = sub < TOPK // (a + 1)
        vals.append(jnp.where(keep, s1[a:a + 1] + s2[0:8], -jnp.inf))
        ids.append(i1[a:a + 1] * n_keys + i2[0:8])
    vals.append(s1[8:TOPK] + s2[0:1])
    ids.append(i1[8:TOPK] * n_keys + i2[0:1])
    return jnp.concatenate(vals, axis=0), jnp.concatenate(ids, axis=0)


def _retrieve_kernel(q_ref, keys_ref, g_ref, idx_ref, gs_ref, is_ref):
    n_keys = keys_ref.shape[2]

    tokens = q_ref.shape[1]

    def head_pair(hp, carry):
        cands, cidxs = [], []
        for hh in range(2):
            h = 2 * hp + hh
            s = jnp.concatenate(
                [lax.dot_general(keys_ref[h, p], q_ref[2 * h + p], NT_DIMS,
                                 preferred_element_type=F32) for p in range(2)],
                axis=1)
            sv, si = _topk_rows(s, TOPK)
            cand, cidx = _pair_candidates(sv[:, :tokens], si[:, :tokens],
                                          sv[:, tokens:], si[:, tokens:], float(n_keys))
            cands.append(cand)
            cidxs.append(cidx)
        sf, eidx = _topk_rows(jnp.concatenate(cands, axis=1), TOPK,
                              payload=jnp.concatenate(cidxs, axis=1))
        e = jnp.exp(sf - sf[0:1])
        g = e / jnp.sum(e, axis=0, keepdims=True)
        for hh in range(2):
            rows = pl.ds(pl.multiple_of((2 * hp + hh) * TOPK, TOPK), TOPK)
            lanes = slice(hh * tokens, (hh + 1) * tokens)
            gs_ref[rows, :] = g[:, lanes]
            is_ref[rows, :] = eidx[:, lanes].astype(jnp.int32)
        return carry

    assert PEER_HEADS % 2 == 0
    lax.fori_loop(0, PEER_HEADS // 2, head_pair, 0)
    g_ref[...] = gs_ref[...].T
    for blk in range(idx_ref.shape[0]):
        idx_ref[blk] = is_ref[:, blk * LANES:(blk + 1) * LANES]


RETRIEVE_BLOCKS = 2


def _retrieve(qp, keys_bf, tb):
    nq, n, _ = qp.shape
    npick = PEER_HEADS * TOPK
    assert tb == LANES and npick == LANES
    nblk = n // tb
    per_step = RETRIEVE_BLOCKS if nblk % RETRIEVE_BLOCKS == 0 else 1
    ts = per_step * tb
    return pl.pallas_call(
        _retrieve_kernel,
        grid=(nblk // per_step,),
        in_specs=[pl.BlockSpec((nq, ts, LANES), lambda i: (0, i, 0)),
                  pl.BlockSpec(keys_bf.shape, lambda i: (0, 0, 0, 0))],
        out_specs=[pl.BlockSpec((ts, npick), lambda i: (i, 0)),
                   pl.BlockSpec((per_step, npick, tb), lambda i: (i, 0, 0))],
        out_shape=[jax.ShapeDtypeStruct((n, npick), F32),
                   jax.ShapeDtypeStruct((nblk, npick, tb), jnp.int32)],
        scratch_shapes=[pltpu.VMEM((npick, ts), F32), pltpu.VMEM((npick, ts), jnp.int32)],
        compiler_params=_params("parallel"),
        name="retrieve",
    )(qp, keys_bf)


N_SLOTS = 8
BITREV3 = (0, 4, 2, 6, 1, 5, 3, 7)
DMA_THREADS = 2


def _peer_kernel(idx_hbm, uv_hbm, g_ref, h2_ref, h2n_ref, xl_ref, gt2_ref, gf_ref, o_ref,
                 idx_smem, idx_sem, sem, peer_ref, stage_ref, wb0, wb1, part0, part1, *bufs):
    tb, d = h2_ref.shape
    npick = g_ref.shape[1]
    nc = d // LANES
    blk_words = npick * tb
    i = pl.program_id(0)
    nblk = pl.num_programs(0)
    cur = i % 2
    nxt = 1 - cur
    has_next = i + 1 < nblk
    ahead = N_SLOTS - 1

    def idx_copy(blk, half):
        return pltpu.make_async_copy(
            idx_hbm.at[pl.ds(pl.multiple_of(blk * blk_words, blk_words), blk_words)],
            idx_smem.at[pl.ds(pl.multiple_of(half * blk_words, blk_words), blk_words)],
            idx_sem.at[half])

    def slab_copy(word, k, slot):
        return pltpu.make_async_copy(uv_hbm.at[idx_smem[word]], bufs[slot].at[k], sem.at[slot])

    def wait(slot):
        pltpu.make_async_copy(uv_hbm.at[pl.ds(0, npick)], bufs[slot], sem.at[slot]).wait()

    sub = lax.broadcasted_iota(jnp.int32, (8, LANES), 0)

    def fold(a, b, dist):
        m = (sub & dist) == 0
        if 2 * dist == 8:
            return jnp.where(m, a, b) + pltpu.roll(jnp.where(m, b, a), dist, 0)
        return (jnp.where(m, a, b)
                + jnp.where(m, pltpu.roll(a, 8 - dist, 0), pltpu.roll(b, dist, 0)))

    def u_side(xrow, slot, part_ref, start_copy=None):
        x = jnp.concatenate([xrow[:, c * LANES:(c + 1) * LANES] for c in range(nc)], axis=0)

        def partial(k):
            p = bufs[slot][k, 0:nc, :].astype(F32) * x
            acc = p[0:8]
            for r in range(1, nc // 8):
                acc = acc + p[8 * r:8 * r + 8]
            return acc

        for grp in range(npick // 8):
            if start_copy is not None:
                for j in range(4):
                    start_copy(grp * 4 + j)
            ps = [partial(grp * 8 + BITREV3[j]) for j in range(8)]
            q = [fold(ps[2 * j], ps[2 * j + 1], 4) for j in range(4)]
            part_ref[grp * 8:grp * 8 + 8, :] = fold(fold(q[0], q[1], 2), fold(q[2], q[3], 2), 1)

    def weights(grow, part_ref, wb_ref):
        a = jnp.sum(part_ref[...].T, axis=0, keepdims=True)
        gelu = 0.5 * a * (1.0 + lax.erf(a * math.sqrt(0.5)))
        wb_ref[...] = jnp.broadcast_to(grow * gelu, (npick, npick)).T

    def v_side(u, slot, wb_ref, start_copy):
        accs = [None] * 4
        for k in range(npick):
            if k % 2 == 0:
                start_copy(npick // 2 + k // 2)
            term = wb_ref[k:k + 1, :] * bufs[slot][k, nc:2 * nc, :].astype(F32)
            accs[k % 4] = term if accs[k % 4] is None else accs[k % 4] + term
        out = (accs[0] + accs[1]) + (accs[2] + accs[3])

        for c in range(nc):
            stage_ref[u:u + 1, c * LANES:(c + 1) * LANES] = out[c:c + 1, :]

    parts = (part0, part1)
    wbs = (wb0, wb1)

    @pl.when(i == 0)
    def _():
        first = idx_copy(0, 0)
        first.start()
        first.wait()
        for t in range(ahead):
            for k in range(npick):
                slab_copy(k * tb + t, k, t).start()
        wait(0)
        u_side(h2_ref[0:1, :], 0, parts[0])

    @pl.when(has_next)
    def _():
        idx_copy(i + 1, nxt).start()

    next_base = jnp.where(has_next, nxt, cur) * blk_words - tb

    last = tb // N_SLOTS - 1

    def row_of(ref, next_ref, r, u, reach):
        row = ref[pl.ds(jnp.minimum(r, tb - 1), 1), :]
        over = u + reach - N_SLOTS
        if over >= 0:
            row = jnp.where(r < tb, row, next_ref[over:over + 1, :])
        return row

    def group_of_tokens(j, carry):
        @pl.when(jnp.logical_and(j == last, has_next))
        def _():
            idx_copy(i + 1, nxt).wait()

        for u in range(N_SLOTS):
            t = j * N_SLOTS + u
            ta = t + ahead
            word0 = jnp.where(ta < tb, cur * blk_words, next_base) + ta
            fill = (u + ahead) % N_SLOTS

            def start_copy(k, word0=word0, fill=fill):
                slab_copy(word0 + k * tb, k, fill).start(priority=k % DMA_THREADS)

            wait((u + 1) % N_SLOTS)
            weights(g_ref[pl.ds(t, 1), :], parts[u % 2], wbs[u % 2])
            u_side(row_of(h2_ref, h2n_ref, t + 1, u, 1), (u + 1) % N_SLOTS, parts[(u + 1) % 2],
                   start_copy)
            v_side(u, u, wbs[u % 2], start_copy)
        peer_ref[pl.ds(pl.multiple_of(j * N_SLOTS, N_SLOTS), N_SLOTS), :] = stage_ref[...]
        return carry

    lax.fori_loop(0, last + 1, group_of_tokens, 0)

    @pl.when(jnp.logical_not(has_next))
    def _():
        for s in range(1, ahead):
            wait(s)

    y = xl_ref[...] + gt2_ref[0] * peer_ref[...]
    o_ref[...] = _rms(y, gf_ref[...])


def _peer(idx_t, uv, g, h2, xl, gt2, gf, blocks_per_batch):
    nblk, npick, tb = idx_t.shape
    n, d = h2.shape
    nc = d // LANES
    assert tb % N_SLOTS == 0 and N_SLOTS % 8 == 0 and npick % 8 == 0 and nc % 16 == 0
    assert uv.shape[1:] == (2 * nc, LANES) and uv.dtype == BF16
    row = lambda i: (i, 0)
    next_rows = lambda i: (jnp.minimum(i + 1, nblk - 1) * (tb // 8), 0)
    return pl.pallas_call(
        _peer_kernel,
        grid=(nblk,),
        in_specs=[pl.BlockSpec(memory_space=pl.ANY), pl.BlockSpec(memory_space=pl.ANY),
                  pl.BlockSpec((tb, npick), row),
                  pl.BlockSpec((tb, d), row), pl.BlockSpec((8, d), next_rows),
                  pl.BlockSpec((tb, d), row),
                  pl.BlockSpec((1, 1, d), lambda i: (i // blocks_per_batch, 0, 0)),
                  pl.BlockSpec((1, d), lambda i: (0, 0))],
        out_specs=pl.BlockSpec((tb, d), row),
        out_shape=jax.ShapeDtypeStruct((n, d), F32),
        scratch_shapes=[pltpu.SMEM((2 * npick * tb,), jnp.int32), pltpu.SemaphoreType.DMA((2,)),
                        pltpu.SemaphoreType.DMA((N_SLOTS,)), pltpu.VMEM((tb, d), F32),
                        pltpu.VMEM((N_SLOTS, d), F32),
                        pltpu.VMEM((npick, npick), F32), pltpu.VMEM((npick, npick), F32),
                        pltpu.VMEM((npick, LANES), F32), pltpu.VMEM((npick, LANES), F32)]
        + [pltpu.VMEM((npick, 2 * nc, LANES), BF16) for _ in range(N_SLOTS)],
        compiler_params=_params("arbitrary"),
        name="peer",
    )(idx_t.reshape(-1), uv, g, h2, h2, xl, gt2, gf)


def _rope_tables(length):
    rows = length // GRID_W
    row = jnp.broadcast_to(jnp.arange(rows)[:, None], (rows, GRID_W)).reshape(-1)
    col = jnp.broadcast_to(jnp.arange(GRID_W)[None, :], (rows, GRID_W)).reshape(-1)
    inv_freq = ROPE_THETA ** (-jnp.arange(ROPE_FREQS, dtype=F32) / ROPE_FREQS)
    ar = row.astype(F32)[:, None] * inv_freq
    ac = col.astype(F32)[:, None] * inv_freq
    cos = jnp.concatenate([jnp.cos(ar), jnp.cos(ar), jnp.cos(ac), jnp.cos(ac)], axis=-1)
    sin = jnp.concatenate([-jnp.sin(ar), jnp.sin(ar), -jnp.sin(ac), jnp.sin(ac)], axis=-1)
    return cos, sin


def kernel(x, c, ctx, c_ctx, w_ada, b_ada, g_norm1, w_in, g_q, g_k, w_fourier, b_fourier,
           w_out, g_norm2, w_query, sub_keys, u_experts, v_experts, g_final):
    b, l, d = x.shape
    assert w_ada.shape[0] == 1, "single-layer configuration only"
    layer = 0
    tb = LANES

    rows = -(-(b + 1) // 8) * 8
    cc = jnp.concatenate([c, c_ctx[None, :], jnp.zeros((rows - b - 1, d), F32)], axis=0)
    mod = _ada(cc, w_ada[layer], b_ada[layer])
    sh1, sc1, gt1, sh2, sc2, gt2 = [m.reshape(b, 1, d) for m in jnp.split(mod[:b], N_MOD, axis=-1)]
    csh1, csc1 = [jnp.broadcast_to(m.reshape(1, 1, d), (b, 1, d))
                  for m in jnp.split(mod[b], N_MOD, axis=-1)[:2]]

    cos, sin = _rope_tables(l)
    g1 = g_norm1[layer].reshape(1, d)
    gq = g_q[layer].reshape(1, HEAD_DIM)
    gk = g_k[layer].reshape(1, HEAD_DIM)
    w_in_bf = w_in[layer].astype(BF16)
    q, k_l, v_l, f = _inproj(x, g1, sh1, sc1, w_in_bf, gq, gk, cos, sin, latent=True)
    lc = ctx.shape[1]
    k_c, v_c = _inproj(ctx, g1, csh1, csc1, w_in_bf, gq, gk, cos[:lc], sin[:lc], latent=False)

    attn = _attention(q, jnp.concatenate([k_l, k_c], axis=1), jnp.concatenate([v_l, v_c], axis=1))

    dft, chan = _dft_tables(l, f.shape[2] // F_GROUPS)
    four = _fourier(f, _fmat(chan, w_fourier[layer]), dft, b_fourier[layer])

    xl, h2, qp = _outproj(attn, four, x, w_out[layer].astype(BF16), gt1,
                          g_norm2[layer].reshape(1, d), sh2, sc2, w_query[layer].astype(BF16))

    g, idx_t = _retrieve(qp, sub_keys[layer].astype(BF16), tb)
    n_exp = u_experts.shape[1]
    uv = jnp.concatenate([u_experts[layer].astype(BF16).reshape(n_exp, d // LANES, LANES),
                          v_experts[layer].astype(BF16).reshape(n_exp, d // LANES, LANES)], axis=1)
    out = _peer(idx_t, uv, g, h2.reshape(b * l, d), xl.reshape(b * l, d), gt2,
                g_final.reshape(1, d), l // tb)
    return out.reshape(b, l, d)
```

```python
import functools
import math

import numpy as np
import jax
import jax.numpy as jnp
from jax import lax
from jax.experimental import pallas as pl
from jax.experimental.pallas import tpu as pltpu

GRID_W = 64
HEAD_DIM = 128
N_Q_HEADS = 8
N_KV_HEADS = 2
GQA_GROUP = N_Q_HEADS // N_KV_HEADS
ATTN_WIDTH = N_Q_HEADS * HEAD_DIM
KV_WIDTH = N_KV_HEADS * HEAD_DIM
ROPE_THETA = 10000.0
ROPE_FREQS = HEAD_DIM // 4
F_GROUPS = 4
PEER_HEADS = 8
TOPK = 16
N_MOD = 6
EPS = 1e-6

LANES = 128
VMEM_LIMIT = 56 * 1024 * 1024

F32 = jnp.float32
BF16 = jnp.bfloat16
NT_DIMS = (((1,), (1,)), ((), ()))


def _params(*sem):
    return pltpu.CompilerParams(dimension_semantics=sem, vmem_limit_bytes=VMEM_LIMIT)


def _resident(shape, index_map):
    return pl.BlockSpec(shape, index_map, pipeline_mode=pl.Buffered(1))


def _rms(x, g):
    return x * lax.rsqrt(jnp.mean(x * x, axis=-1, keepdims=True) + EPS) * g


def _ada_kernel(c_ref, w_ref, b_ref, o_ref):
    c = c_ref[...]
    a = (c * jax.nn.sigmoid(c)).astype(BF16)
    o_ref[...] = jnp.dot(a, w_ref[...].astype(BF16), preferred_element_type=F32) + b_ref[...]


def _ada(cc, w, b):
    rows, d = cc.shape
    n = w.shape[1]
    tn = next(t for t in (1024, 512, 256, LANES) if n % t == 0)
    return pl.pallas_call(
        _ada_kernel,
        grid=(n // tn,),
        in_specs=[pl.BlockSpec((rows, d), lambda j: (0, 0)),
                  pl.BlockSpec((d, tn), lambda j: (0, j)),
                  pl.BlockSpec((1, tn), lambda j: (0, j))],
        out_specs=pl.BlockSpec((rows, tn), lambda j: (0, j)),
        out_shape=jax.ShapeDtypeStruct((rows, n), F32),
        compiler_params=_params("parallel"),
        name="ada",
    )(cc, w, b.reshape(1, n))


def _rope(t, cos, sin_signed):
    lane = lax.broadcasted_iota(jnp.int32, t.shape, 1)
    partner = jnp.where((lane % 64) < 32, pltpu.roll(t, 96, 1), pltpu.roll(t, 32, 1))
    return t * cos + partner * sin_signed


def _inproj_kernel(x_ref, g1_ref, sh_ref, sc_ref, w_ref, gq_ref, gk_ref, cos_ref, sin_ref,
                   *out_refs, latent):
    h = (_rms(x_ref[0], g1_ref[...]) * (1.0 + sc_ref[0]) + sh_ref[0]).astype(BF16)
    if latent:
        q_ref, k_ref, v_ref, f_ref = out_refs
        cos, sin = cos_ref[...], sin_ref[...]
        q = jnp.dot(h, w_ref[:, 0:ATTN_WIDTH], preferred_element_type=F32)
        scale = HEAD_DIM ** -0.5
        for j in range(N_Q_HEADS):
            sl = slice(j * HEAD_DIM, (j + 1) * HEAD_DIM)
            q_ref[0, :, sl] = (_rope(_rms(q[:, sl], gq_ref[...]), cos, sin) * scale).astype(BF16)
        k0 = ATTN_WIDTH
    else:
        k_ref, v_ref = out_refs
        k0 = 0
    k = jnp.dot(h, w_ref[:, k0:k0 + KV_WIDTH], preferred_element_type=F32)
    for j in range(N_KV_HEADS):
        sl = slice(j * HEAD_DIM, (j + 1) * HEAD_DIM)
        t = _rms(k[:, sl], gk_ref[...])
        if latent:
            t = _rope(t, cos, sin)
        k_ref[0, :, sl] = t.astype(BF16)
    v_ref[0] = jnp.dot(h, w_ref[:, k0 + KV_WIDTH:k0 + 2 * KV_WIDTH],
                       preferred_element_type=F32).astype(BF16)
    if latent:
        f_ref[0] = jnp.dot(h, w_ref[:, ATTN_WIDTH + 2 * KV_WIDTH:],
                           preferred_element_type=F32).astype(BF16)


def _inproj(x, g1, sh, sc, w_bf, gq, gk, cos, sin, latent):
    b, l, d = x.shape
    tm = min(l, 512)
    wcols = w_bf.shape[1]
    fw = wcols - ATTN_WIDTH - 2 * KV_WIDTH
    row = lambda bi, i: (bi, i, 0)
    per_b = pl.BlockSpec((1, 1, d), lambda bi, i: (bi, 0, 0))
    vec = lambda n: pl.BlockSpec((1, n), lambda bi, i: (0, 0))
    if latent:
        w_spec = _resident((d, wcols), lambda bi, i: (0, 0))
        widths = (ATTN_WIDTH, KV_WIDTH, KV_WIDTH, fw)
    else:
        assert ATTN_WIDTH % (2 * KV_WIDTH) == 0
        w_spec = _resident((d, 2 * KV_WIDTH), lambda bi, i: (0, ATTN_WIDTH // (2 * KV_WIDTH)))
        widths = (KV_WIDTH, KV_WIDTH)
    return pl.pallas_call(
        functools.partial(_inproj_kernel, latent=latent),
        grid=(b, l // tm),
        in_specs=[pl.BlockSpec((1, tm, d), row), vec(d), per_b, per_b, w_spec,
                  vec(HEAD_DIM), vec(HEAD_DIM),
                  pl.BlockSpec((tm, HEAD_DIM), lambda bi, i: (i, 0)),
                  pl.BlockSpec((tm, HEAD_DIM), lambda bi, i: (i, 0))],
        out_specs=[pl.BlockSpec((1, tm, n), row) for n in widths],
        out_shape=[jax.ShapeDtypeStruct((b, l, n), BF16) for n in widths],
        compiler_params=_params("parallel", "parallel"),
        name="inproj_latent" if latent else "inproj_ctx",
    )(x, g1, sh, sc, w_bf, gq, gk, cos, sin)


def _attn_kernel(q_ref, k_ref, v_ref, o_ref):
    k = k_ref[0]
    v = v_ref[0]
    for j in range(GQA_GROUP):
        sl = slice(j * HEAD_DIM, (j + 1) * HEAD_DIM)
        s = lax.dot_general(q_ref[0, :, sl], k, NT_DIMS, preferred_element_type=F32)
        p = jnp.exp(s - jnp.max(s, axis=-1, keepdims=True))
        denom = jnp.sum(p, axis=-1, keepdims=True)
        o = jnp.dot(p.astype(BF16), v, preferred_element_type=F32) / denom
        o_ref[0, :, sl] = o.astype(BF16)


def _attention(q, k_all, v_all):
    b, l, _ = q.shape
    lk = k_all.shape[1]
    tq = min(l, 256)
    gw = GQA_GROUP * HEAD_DIM
    return pl.pallas_call(
        _attn_kernel,
        grid=(b, N_KV_HEADS, l // tq),
        in_specs=[pl.BlockSpec((1, tq, gw), lambda bi, g, i: (bi, i, g)),
                  pl.BlockSpec((1, lk, HEAD_DIM), lambda bi, g, i: (bi, 0, g)),
                  pl.BlockSpec((1, lk, HEAD_DIM), lambda bi, g, i: (bi, 0, g))],
        out_specs=pl.BlockSpec((1, tq, gw), lambda bi, g, i: (bi, i, g)),
        out_shape=jax.ShapeDtypeStruct((b, l, ATTN_WIDTH), BF16),
        compiler_params=_params("parallel", "parallel", "parallel"),
        name="attn",
    )(q, k_all, v_all)


def _dft_tables(l, c):
    kl = (np.arange(l, dtype=np.int64)[:, None] * np.arange(l, dtype=np.int64)[None, :]) % l
    ang = 2.0 * np.pi * kl.astype(np.float64) / l
    pos = np.concatenate([np.cos(ang), -np.sin(ang)], axis=1)
    jc = (np.arange(c, dtype=np.int64)[:, None] * np.arange(c, dtype=np.int64)[None, :]) % c
    angc = 2.0 * np.pi * jc.astype(np.float64) / c
    chan = np.stack([np.cos(angc), np.sin(angc)]) / math.sqrt(l * c)

    def split(t):
        head = t.astype(np.float32)
        return jnp.asarray(head) + jnp.asarray((t - head).astype(np.float32))

    return split(pos).astype(BF16), split(chan)


def _fmat_kernel(cs_ref, w_ref, o_ref):
    c = w_ref.shape[-1]
    w = w_ref[0]
    o_ref[0, :, 0:c] = jnp.dot(cs_ref[0], w, preferred_element_type=F32,
                               precision=lax.Precision.HIGHEST).astype(BF16)
    o_ref[0, :, c:2 * c] = jnp.dot(cs_ref[1], w, preferred_element_type=F32,
                                   precision=lax.Precision.HIGHEST).astype(BF16)


def _fmat(chan, w_f):
    g, c, _ = w_f.shape
    return pl.pallas_call(
        _fmat_kernel,
        grid=(g,),
        in_specs=[pl.BlockSpec((2, c, c), lambda gi: (0, 0, 0)),
                  pl.BlockSpec((1, c, c), lambda gi: (gi, 0, 0))],
        out_specs=pl.BlockSpec((1, c, 2 * c), lambda gi: (gi, 0, 0)),
        out_shape=jax.ShapeDtypeStruct((g, c, 2 * c), BF16),
        compiler_params=_params("parallel"),
        name="fmat",
    )(chan, w_f)


def _fourier_kernel(x_ref, m_ref, dft_ref, b_ref, o_ref, z_ref, *, tr):
    l, c = x_ref.shape[1], x_ref.shape[2]
    z = jnp.dot(x_ref[0], m_ref[0], preferred_element_type=F32)
    z_ref[0:l, :] = z[:, 0:c].astype(BF16)
    z_ref[l:2 * l, :] = z[:, c:2 * c].astype(BF16)
    for r in range(l // tr):
        rows = slice(r * tr, (r + 1) * tr)
        y = jnp.dot(dft_ref[rows, :], z_ref[...], preferred_element_type=F32) + b_ref[0]
        o_ref[0, rows, :] = y.astype(BF16)


def _fourier(f, fm, dft, b_f):
    b, l, fw = f.shape
    c = fw // F_GROUPS
    tr = min(l, 512)
    return pl.pallas_call(
        functools.partial(_fourier_kernel, tr=tr),
        grid=(b, F_GROUPS),
        in_specs=[pl.BlockSpec((1, l, c), lambda bi, g: (bi, 0, g)),
                  pl.BlockSpec((1, c, 2 * c), lambda bi, g: (g, 0, 0)),
                  _resident((l, 2 * l), lambda bi, g: (0, 0)),
                  pl.BlockSpec((1, 1, c), lambda bi, g: (g, 0, 0))],
        out_specs=pl.BlockSpec((1, l, c), lambda bi, g: (bi, 0, g)),
        out_shape=jax.ShapeDtypeStruct((b, l, fw), BF16),
        scratch_shapes=[pltpu.VMEM((2 * l, c), BF16)],
        compiler_params=_params("parallel", "parallel"),
        name="fourier",
    )(f, fm, dft, b_f.reshape(F_GROUPS, 1, c))


def _outproj_kernel(a_ref, f_ref, x_ref, wo_ref, gt1_ref, g2_ref, sh2_ref, sc2_ref, wq_ref, keys_ref,
                    xl_ref, h2_ref, g_ref, idx_ref, qp_ref, gs_ref, is_ref):
    s = pl.program_id(0)

    @pl.when(s == 0)
    def _():
        qp_ref[...] = jnp.zeros_like(qp_ref)

    _route(qp_ref.at[(s + 1) % 2], keys_ref, g_ref, idx_ref, gs_ref, is_ref)

    aw = a_ref.shape[2]
    y = jnp.dot(a_ref[0], wo_ref[0:aw, :], preferred_element_type=F32)
    y = y + jnp.dot(f_ref[0], wo_ref[aw:, :], preferred_element_type=F32)
    xl = x_ref[0] + gt1_ref[0] * y
    xl_ref[0] = xl
    h2 = _rms(xl, g2_ref[...]) * (1.0 + sc2_ref[0]) + sh2_ref[0]
    h2_ref[0] = h2
    qp = jnp.dot(h2.astype(BF16), wq_ref[...], preferred_element_type=F32)
    for j in range(qp_ref.shape[1]):
        qp_ref[s % 2, j] = qp[:, j * LANES:(j + 1) * LANES].astype(BF16)


def _outproj(attn, four, x, wo_bf, gt1, g2, sh2, sc2, wq_bf, keys_bf, tb):
    b, l, d = x.shape
    tm = min(l, 256)
    nt = l // tm
    steps = b * nt
    aw, fw = attn.shape[2], four.shape[2]
    qw = wq_bf.shape[1]
    npick = PEER_HEADS * TOPK
    assert tb == LANES and npick == LANES and tm % tb == 0 and qw == 2 * PEER_HEADS * LANES
    tile = lambda s: jnp.minimum(s, steps - 1)
    row = lambda s: (tile(s) // nt, tile(s) % nt, 0)
    per_b = pl.BlockSpec((1, 1, d), lambda s: (tile(s) // nt, 0, 0))
    prev = lambda s: jnp.maximum(s - 1, 0)
    return pl.pallas_call(
        _outproj_kernel,
        grid=(steps + 1,),
        in_specs=[pl.BlockSpec((1, tm, aw), row), pl.BlockSpec((1, tm, fw), row),
                  pl.BlockSpec((1, tm, d), row),
                  _resident((aw + fw, d), lambda s: (0, 0)),
                  per_b, pl.BlockSpec((1, d), lambda s: (0, 0)), per_b, per_b,
                  _resident((d, qw), lambda s: (0, 0)),
                  pl.BlockSpec(keys_bf.shape, lambda s: (0, 0, 0, 0))],
        out_specs=[pl.BlockSpec((1, tm, d), row), pl.BlockSpec((1, tm, d), row),
                   pl.BlockSpec((tm, npick), lambda s: (prev(s), 0)),
                   pl.BlockSpec((tm // tb, npick, tb), lambda s: (prev(s), 0, 0))],
        out_shape=[jax.ShapeDtypeStruct((b, l, d), F32), jax.ShapeDtypeStruct((b, l, d), F32),
                   jax.ShapeDtypeStruct((b * l, npick), F32),
                   jax.ShapeDtypeStruct((b * l // tb, npick, tb), jnp.int32)],
        scratch_shapes=[pltpu.VMEM((2, qw // LANES, tm, LANES), BF16),
                        pltpu.VMEM((npick, tm), F32), pltpu.VMEM((npick, tm), jnp.int32)],
        compiler_params=_params("arbitrary"),
        name="outproj",
    )(attn, four, x, wo_bf, gt1, g2, sh2, sc2, wq_bf, keys_bf)


def _topk_rows(s, k, payload=None):
    rows, lanes = s.shape
    assert rows % 8 == 0
    sub = lax.broadcasted_iota(jnp.int32, (8, lanes), 0).astype(F32)
    vals = [s[8 * g:8 * g + 8] for g in range(rows // 8)]
    ids = [sub + float(8 * g) for g in range(rows // 8)]
    pay = None if payload is None else [payload[8 * g:8 * g + 8] for g in range(rows // 8)]
    out_v, out_s = [], []
    for _ in range(k):
        v, i = vals, ids
        while len(v) > 1:
            nv, ni = [], []
            for a in range(0, len(v) - 1, 2):
                nv.append(jnp.maximum(v[a], v[a + 1]))
                ni.append(jnp.where(v[a] >= v[a + 1], i[a], i[a + 1]))
            if len(v) % 2:
                nv.append(v[-1])
                ni.append(i[-1])
            v, i = nv, ni
        m = jnp.max(v[0], axis=0, keepdims=True)
        win = jnp.min(jnp.where(v[0] == m, i[0], float(rows)), axis=0, keepdims=True)
        hits = [g == win for g in ids]
        out_v.append(m)
        if pay is None:
            out_s.append(win)
        else:
            picked = [jnp.where(h, p, 0.0) for h, p in zip(hits, pay)]
            out_s.append(jnp.sum(functools.reduce(jnp.add, picked), axis=0, keepdims=True))
        vals = [jnp.where(h, -jnp.inf, g) for h, g in zip(hits, vals)]
    return jnp.concatenate(out_v, axis=0), jnp.concatenate(out_s, axis=0)


def _pair_candidates(s1, i1, s2, i2, n_keys):
    assert TOPK == 16
    sub = lax.broadcasted_iota(jnp.int32, (8, s1.shape[1]), 0)
    ids = [i1[0:1] * n_keys + i2, i1[1:2] * n_keys + i2[0:8]]
    vals = [s1[0:1] + s2, s1[1:2] + s2[0:8]]
    for a in range(2, 8):
        keep = sub < TOPK // (a + 1)
        vals.append(jnp.where(keep, s1[a:a + 1] + s2[0:8], -jnp.inf))
        ids.append(i1[a:a + 1] * n_keys + i2[0:8])
    vals.append(s1[8:TOPK] + s2[0:1])
    ids.append(i1[8:TOPK] * n_keys + i2[0:1])
    return jnp.concatenate(vals, axis=0), jnp.concatenate(ids, axis=0)


def _route(q_ref, keys_ref, g_ref, idx_ref, gs_ref, is_ref):
    n_keys = keys_ref.shape[2]
    tokens = gs_ref.shape[1]

    assert PEER_HEADS % 2 == 0
    for hp in range(PEER_HEADS // 2):
        cands, cidxs = [], []
        for hh in range(2):
            h = 2 * hp + hh
            s = jnp.concatenate(
                [lax.dot_general(keys_ref[h, p], q_ref[2 * h + p], NT_DIMS,
                                 preferred_element_type=F32) for p in range(2)],
                axis=1)
            sv, si = _topk_rows(s, TOPK)
            cand, cidx = _pair_candidates(sv[:, :tokens], si[:, :tokens],
                                          sv[:, tokens:], si[:, tokens:], float(n_keys))
            cands.append(cand)
            cidxs.append(cidx)
        sf, eidx = _topk_rows(jnp.concatenate(cands, axis=1), TOPK,
                              payload=jnp.concatenate(cidxs, axis=1))
        e = jnp.exp(sf - sf[0:1])
        g = e / jnp.sum(e, axis=0, keepdims=True)
        for hh in range(2):
            rows = slice((2 * hp + hh) * TOPK, (2 * hp + hh + 1) * TOPK)
            lanes = slice(hh * tokens, (hh + 1) * tokens)
            gs_ref[rows, :] = g[:, lanes]
            is_ref[rows, :] = eidx[:, lanes].astype(jnp.int32)
    g_ref[...] = gs_ref[...].T
    for blk in range(idx_ref.shape[0]):
        idx_ref[blk] = is_ref[:, blk * LANES:(blk + 1) * LANES]


N_SLOTS = 8
BITREV3 = (0, 4, 2, 6, 1, 5, 3, 7)
DMA_THREADS = 2


def _peer_kernel(idx_hbm, uv_hbm, g_ref, h2_ref, h2n_ref, xl_ref, gt2_ref, gf_ref, o_ref,
                 idx_smem, idx_sem, sem, peer_ref, stage_ref, wb0, wb1, part0, part1, *bufs):
    tb, d = h2_ref.shape
    npick = g_ref.shape[1]
    nc = d // LANES
    blk_words = npick * tb
    i = pl.program_id(0)
    nblk = pl.num_programs(0)
    cur = i % 2
    nxt = 1 - cur
    has_next = i + 1 < nblk
    ahead = N_SLOTS - 1

    def idx_copy(blk, half):
        return pltpu.make_async_copy(
            idx_hbm.at[pl.ds(pl.multiple_of(blk * blk_words, blk_words), blk_words)],
            idx_smem.at[pl.ds(pl.multiple_of(half * blk_words, blk_words), blk_words)],
            idx_sem.at[half])

    def slab_copy(word, k, slot):
        return pltpu.make_async_copy(uv_hbm.at[idx_smem[word]], bufs[slot].at[k], sem.at[slot])

    def wait(slot):
        pltpu.make_async_copy(uv_hbm.at[pl.ds(0, npick)], bufs[slot], sem.at[slot]).wait()

    sub = lax.broadcasted_iota(jnp.int32, (8, LANES), 0)

    def fold(a, b, dist):
        m = (sub & dist) == 0
        if 2 * dist == 8:
            return jnp.where(m, a, b) + pltpu.roll(jnp.where(m, b, a), dist, 0)
        return (jnp.where(m, a, b)
                + jnp.where(m, pltpu.roll(a, 8 - dist, 0), pltpu.roll(b, dist, 0)))

    def u_side(xrow, slot, part_ref, start_copy=None):
        x = jnp.concatenate([xrow[:, c * LANES:(c + 1) * LANES] for c in range(nc)], axis=0)

        def partial(k):
            p = bufs[slot][k, 0:nc, :].astype(F32) * x
            acc = p[0:8]
            for r in range(1, nc // 8):
                acc = acc + p[8 * r:8 * r + 8]
            return acc

        for grp in range(npick // 8):
            if start_copy is not None:
                for j in range(4):
                    start_copy(grp * 4 + j)
            ps = [partial(grp * 8 + BITREV3[j]) for j in range(8)]
            q = [fold(ps[2 * j], ps[2 * j + 1], 4) for j in range(4)]
            part_ref[grp * 8:grp * 8 + 8, :] = fold(fold(q[0], q[1], 2), fold(q[2], q[3], 2), 1)

    def weights(grow, part_ref, wb_ref):
        a = jnp.sum(part_ref[...].T, axis=0, keepdims=True)
        gelu = 0.5 * a * (1.0 + lax.erf(a * math.sqrt(0.5)))
        wb_ref[...] = jnp.broadcast_to(grow * gelu, (npick, npick)).T

    def v_side(u, slot, wb_ref, start_copy):
        accs = [None] * 4
        for k in range(npick):
            if k % 2 == 0:
                start_copy(npick // 2 + k // 2)
            term = wb_ref[k:k + 1, :] * bufs[slot][k, nc:2 * nc, :].astype(F32)
            accs[k % 4] = term if accs[k % 4] is None else accs[k % 4] + term
        out = (accs[0] + accs[1]) + (accs[2] + accs[3])

        for c in range(nc):
            stage_ref[u:u + 1, c * LANES:(c + 1) * LANES] = out[c:c + 1, :]

    parts = (part0, part1)
    wbs = (wb0, wb1)

    @pl.when(i == 0)
    def _():
        first = idx_copy(0, 0)
        first.start()
        first.wait()
        for t in range(ahead):
            for k in range(npick):
                slab_copy(k * tb + t, k, t).start()
        wait(0)
        u_side(h2_ref[0:1, :], 0, parts[0])

    @pl.when(has_next)
    def _():
        idx_copy(i + 1, nxt).start()

    next_base = jnp.where(has_next, nxt, cur) * blk_words - tb

    last = tb // N_SLOTS - 1

    def row_of(ref, next_ref, r, u, reach):
        row = ref[pl.ds(jnp.minimum(r, tb - 1), 1), :]
        over = u + reach - N_SLOTS
        if over >= 0:
            row = jnp.where(r < tb, row, next_ref[over:over + 1, :])
        return row

    def group_of_tokens(j, carry):
        @pl.when(jnp.logical_and(j == last, has_next))
        def _():
            idx_copy(i + 1, nxt).wait()

        for u in range(N_SLOTS):
            t = j * N_SLOTS + u
            ta = t + ahead
            word0 = jnp.where(ta < tb, cur * blk_words, next_base) + ta
            fill = (u + ahead) % N_SLOTS

            def start_copy(k, word0=word0, fill=fill):
                slab_copy(word0 + k * tb, k, fill).start(priority=k % DMA_THREADS)

            wait((u + 1) % N_SLOTS)
            weights(g_ref[pl.ds(t, 1), :], parts[u % 2], wbs[u % 2])
            u_side(row_of(h2_ref, h2n_ref, t + 1, u, 1), (u + 1) % N_SLOTS, parts[(u + 1) % 2],
                   start_copy)
            v_side(u, u, wbs[u % 2], start_copy)
        peer_ref[pl.ds(pl.multiple_of(j * N_SLOTS, N_SLOTS), N_SLOTS), :] = stage_ref[...]
        return carry

    lax.fori_loop(0, last + 1, group_of_tokens, 0)

    @pl.when(jnp.logical_not(has_next))
    def _():
        for s in range(1, ahead):
            wait(s)

    y = xl_ref[...] + gt2_ref[0] * peer_ref[...]
    o_ref[...] = _rms(y, gf_ref[...])


def _peer(idx_t, uv, g, h2, xl, gt2, gf, blocks_per_batch):
    nblk, npick, tb = idx_t.shape
    n, d = h2.shape
    nc = d // LANES
    assert tb % N_SLOTS == 0 and N_SLOTS % 8 == 0 and npick % 8 == 0 and nc % 16 == 0
    assert uv.shape[1:] == (2 * nc, LANES) and uv.dtype == BF16
    row = lambda i: (i, 0)
    next_rows = lambda i: (jnp.minimum(i + 1, nblk - 1) * (tb // 8), 0)
    return pl.pallas_call(
        _peer_kernel,
        grid=(nblk,),
        in_specs=[pl.BlockSpec(memory_space=pl.ANY), pl.BlockSpec(memory_space=pl.ANY),
                  pl.BlockSpec((tb, npick), row),
                  pl.BlockSpec((tb, d), row), pl.BlockSpec((8, d), next_rows),
                  pl.BlockSpec((tb, d), row),
                  pl.BlockSpec((1, 1, d), lambda i: (i // blocks_per_batch, 0, 0)),
                  pl.BlockSpec((1, d), lambda i: (0, 0))],
        out_specs=pl.BlockSpec((tb, d), row),
        out_shape=jax.ShapeDtypeStruct((n, d), F32),
        scratch_shapes=[pltpu.SMEM((2 * npick * tb,), jnp.int32), pltpu.SemaphoreType.DMA((2,)),
                        pltpu.SemaphoreType.DMA((N_SLOTS,)), pltpu.VMEM((tb, d), F32),
                        pltpu.VMEM((N_SLOTS, d), F32),
                        pltpu.VMEM((npick, npick), F32), pltpu.VMEM((npick, npick), F32),
                        pltpu.VMEM((npick, LANES), F32), pltpu.VMEM((npick, LANES), F32)]
        + [pltpu.VMEM((npick, 2 * nc, LANES), BF16) for _ in range(N_SLOTS)],
        compiler_params=_params("arbitrary"),
        name="peer",
    )(idx_t.reshape(-1), uv, g, h2, h2, xl, gt2, gf)


def _rope_tables(length):
    rows = length // GRID_W
    row = jnp.broadcast_to(jnp.arange(rows)[:, None], (rows, GRID_W)).reshape(-1)
    col = jnp.broadcast_to(jnp.arange(GRID_W)[None, :], (rows, GRID_W)).reshape(-1)
    inv_freq = ROPE_THETA ** (-jnp.arange(ROPE_FREQS, dtype=F32) / ROPE_FREQS)
    ar = row.astype(F32)[:, None] * inv_freq
    ac = col.astype(F32)[:, None] * inv_freq
    cos = jnp.concatenate([jnp.cos(ar), jnp.cos(ar), jnp.cos(ac), jnp.cos(ac)], axis=-1)
    sin = jnp.concatenate([-jnp.sin(ar), jnp.sin(ar), -jnp.sin(ac), jnp.sin(ac)], axis=-1)
    return cos, sin


def kernel(x, c, ctx, c_ctx, w_ada, b_ada, g_norm1, w_in, g_q, g_k, w_fourier, b_fourier,
           w_out, g_norm2, w_query, sub_keys, u_experts, v_experts, g_final):
    b, l, d = x.shape
    assert w_ada.shape[0] == 1, "single-layer configuration only"
    layer = 0
    tb = LANES

    rows = -(-(b + 1) // 8) * 8
    cc = jnp.concatenate([c, c_ctx[None, :], jnp.zeros((rows - b - 1, d), F32)], axis=0)
    mod = _ada(cc, w_ada[layer], b_ada[layer])
    sh1, sc1, gt1, sh2, sc2, gt2 = [m.reshape(b, 1, d) for m in jnp.split(mod[:b], N_MOD, axis=-1)]
    csh1, csc1 = [jnp.broadcast_to(m.reshape(1, 1, d), (b, 1, d))
                  for m in jnp.split(mod[b], N_MOD, axis=-1)[:2]]

    cos, sin = _rope_tables(l)
    g1 = g_norm1[layer].reshape(1, d)
    gq = g_q[layer].reshape(1, HEAD_DIM)
    gk = g_k[layer].reshape(1, HEAD_DIM)
    w_in_bf = w_in[layer].astype(BF16)
    q, k_l, v_l, f = _inproj(x, g1, sh1, sc1, w_in_bf, gq, gk, cos, sin, latent=True)
    lc = ctx.shape[1]
    k_c, v_c = _inproj(ctx, g1, csh1, csc1, w_in_bf, gq, gk, cos[:lc], sin[:lc], latent=False)

    attn = _attention(q, jnp.concatenate([k_l, k_c], axis=1), jnp.concatenate([v_l, v_c], axis=1))

    dft, chan = _dft_tables(l, f.shape[2] // F_GROUPS)
    four = _fourier(f, _fmat(chan, w_fourier[layer]), dft, b_fourier[layer])

    xl, h2, g, idx_t = _outproj(attn, four, x, w_out[layer].astype(BF16), gt1,
                                g_norm2[layer].reshape(1, d), sh2, sc2, w_query[layer].astype(BF16),
                                sub_keys[layer].astype(BF16), tb)
    n_exp = u_experts.shape[1]
    uv = jnp.concatenate([u_experts[layer].astype(BF16).reshape(n_exp, d // LANES, LANES),
                          v_experts[layer].astype(BF16).reshape(n_exp, d // LANES, LANES)], axis=1)
    out = _peer(idx_t, uv, g, h2.reshape(b * l, d), xl.reshape(b * l, d), gt2,
                g_final.reshape(1, d), l // tb)
    return out.reshape(b, l, d)
```
